```python
import math
import jax
import jax.numpy as jnp
from jax import lax
import numpy as np

D_MODEL = 1024
BATCH = 4
SEQ = 4096
DEPTH = 4

GRID_W = 64
MEM_LEN = 256
ML_HEADS = 4
ML_WIDTH = D_MODEL // 2
ML_DIM = ML_WIDTH // ML_HEADS
ML_CHUNK = 128
NA_HEADS = 8
NA_WIDTH = D_MODEL - ML_WIDTH
NA_DIM = NA_WIDTH // NA_HEADS
NA_ROWS = 8
NA_COLS = 16
XA_HEADS = 4
XA_DIM = D_MODEL // XA_HEADS
D_FF = 128 * ((8 * D_MODEL // 3 + 127) // 128)
IN_COLS = 4 * ML_WIDTH + 4 * ML_HEADS + 3 * NA_WIDTH
DN_ALPHA = (2.0 * DEPTH) ** 0.25
DN_BETA = (8.0 * DEPTH) ** -0.25
LN_EPS = 1e-5

kernel_name = "hybrid_mlstm_natten_encoder"


def layer_norm(x, w, b):
    xf = x.astype(jnp.float32)
    mu = jnp.mean(xf, axis=-1, keepdims=True)
    var = jnp.mean(jnp.square(xf - mu), axis=-1, keepdims=True)
    y = (xf - mu) * lax.rsqrt(var + LN_EPS) * w.astype(jnp.float32) + b.astype(jnp.float32)
    return y.astype(x.dtype)


def mlstm_scan(q, k, v, ig, fg):
    b, h, t, dh = q.shape
    nc = t // ML_CHUNK

    def chunks(a):
        a = a.reshape((b, h, nc, ML_CHUNK) + a.shape[3:])
        return jnp.moveaxis(a, 2, 0)

    logf = jax.nn.log_sigmoid(fg)
    tril = jnp.tril(jnp.ones((ML_CHUNK, ML_CHUNK), dtype=bool))

    def step(carry, inp):
        c_mat, n_vec, m = carry
        qc, kc, vc, ic, lfc = inp
        a = jnp.cumsum(lfc, axis=-1)
        d_log = jnp.where(tril, a[..., :, None] - a[..., None, :] + ic[..., None, :], -jnp.inf)
        inter_log = a + m[..., None]
        m_t = jnp.maximum(inter_log, jnp.max(d_log, axis=-1))
        w_qk = jnp.exp(d_log - m_t[..., None]) * jnp.einsum('bhtd,bhsd->bhts', qc, kc)
        inter = jnp.exp(inter_log - m_t)
        num = jnp.einsum('bhts,bhsd->bhtd', w_qk, vc) + inter[..., None] * jnp.einsum('bhtd,bhde->bhte', qc, c_mat)
        den = jnp.sum(w_qk, axis=-1) + inter * jnp.einsum('bhtd,bhd->bht', qc, n_vec)
        h_out = num / jnp.maximum(jnp.abs(den), jnp.exp(-m_t))[..., None]
        a_end = a[..., -1]
        up_log = a_end[..., None] - a + ic
        m_new = jnp.maximum(a_end + m, jnp.max(up_log, axis=-1))
        kw = kc * jnp.exp(up_log - m_new[..., None])[..., None]
        decay = jnp.exp(a_end + m - m_new)
        c_new = decay[..., None, None] * c_mat + jnp.einsum('bhsd,bhse->bhde', kw, vc)
        n_new = decay[..., None] * n_vec + jnp.sum(kw, axis=2)
        return (c_new, n_new, m_new), h_out

    init = (jnp.zeros((b, h, dh, dh), jnp.float32),
            jnp.zeros((b, h, dh), jnp.float32),
            jnp.zeros((b, h), jnp.float32))
    _, hs = lax.scan(step, init, (chunks(q), chunks(k), chunks(v), chunks(ig), chunks(logf)))
    return jnp.moveaxis(hs, 0, 2).reshape(b, h, t, dh)


def neighborhood_attention(nq, nk, nv, rpb):
    b, t, _ = nq.shape
    rows = t // GRID_W
    kr = min(NA_ROWS, rows)

    def grid(a):
        return a.reshape(b, rows, GRID_W, NA_HEADS, NA_DIM).transpose(0, 3, 1, 2, 4).astype(jnp.float32)

    qg, kg, vg = grid(nq), grid(nk), grid(nv)
    r = jnp.arange(rows)
    row_idx = jnp.clip(r - kr // 2, 0, rows - kr)[:, None] + jnp.arange(kr)[None, :]
    k_rows = kg[:, :, row_idx].reshape(b, NA_HEADS, rows, kr * GRID_W, NA_DIM)
    v_rows = vg[:, :, row_idx].reshape(b, NA_HEADS, rows, kr * GRID_W, NA_DIM)
    s = jnp.einsum('bhrqd,bhrkd->bhrqk', qg, k_rows) * (NA_DIM ** -0.5)
    j = jnp.arange(GRID_W)
    cs = jnp.clip(j - NA_COLS // 2, 0, GRID_W - NA_COLS)
    col_ok = (j[None, :] >= cs[:, None]) & (j[None, :] < cs[:, None] + NA_COLS)
    dc_idx = jnp.clip(j[None, :] - j[:, None] + NA_COLS - 1, 0, 2 * NA_COLS - 2)
    dr_idx = row_idx - r[:, None] + NA_ROWS - 1
    bias = rpb[:, dr_idx[:, None, :, None], dc_idx[None, :, None, :]]
    bias = bias.reshape(NA_HEADS, rows, GRID_W, kr * GRID_W).astype(jnp.float32)
    mask = jnp.broadcast_to(col_ok[:, None, :], (GRID_W, kr, GRID_W)).reshape(GRID_W, kr * GRID_W)
    s = jnp.where(mask, s + bias, -jnp.inf)
    p = jax.nn.softmax(s, axis=-1)
    o = jnp.einsum('bhrqk,bhrkd->bhrqd', p, v_rows)
    return o.transpose(0, 2, 3, 1, 4).reshape(b, t, NA_WIDTH)


def hybrid_mixer(x, w_in, b_in, ml_norm_w, na_rpb, w_out, b_out):
    b, t, _ = x.shape
    z = x @ w_in + b_in
    sizes = [ML_WIDTH] * 4 + [ML_HEADS] * 4 + [NA_WIDTH] * 3
    offs = [int(o) for o in np.cumsum(sizes)[:-1]]
    mq, mk, mv, mo, i_f, f_f, i_b, f_b, nq, nk, nv = jnp.split(z, offs, axis=-1)

    def heads(a):
        return a.reshape(b, t, ML_HEADS, ML_DIM).transpose(0, 2, 1, 3).astype(jnp.float32)

    def gate(a):
        return a.transpose(0, 2, 1).astype(jnp.float32)

    q, k, v = heads(mq), heads(mk) * (ML_DIM ** -0.5), heads(mv)
    h_fwd = mlstm_scan(q, k, v, gate(i_f), gate(f_f))
    flip = lambda a: jnp.flip(a, axis=2)
    h_bwd = flip(mlstm_scan(flip(q), flip(k), flip(v), flip(gate(i_b)), flip(gate(f_b))))
    hm = h_fwd + h_bwd
    mu = jnp.mean(hm, axis=-1, keepdims=True)
    var = jnp.mean(jnp.square(hm - mu), axis=-1, keepdims=True)
    hn = ((hm - mu) * lax.rsqrt(var + LN_EPS)).transpose(0, 2, 1, 3).reshape(b, t, ML_WIDTH)
    ml_out = jax.nn.sigmoid(mo.astype(jnp.float32)) * (hn * ml_norm_w.astype(jnp.float32))
    na_out = neighborhood_attention(nq, nk, nv, na_rpb)
    y = jnp.concatenate([ml_out, na_out], axis=-1).astype(x.dtype)
    return y @ w_out + b_out


def memory_xattn(x, mem, w_q, w_kv, w_o, b_o):
    b, t, d = x.shape
    m = mem.shape[1]
    q = (x @ w_q).reshape(b, t, XA_HEADS, XA_DIM).astype(jnp.float32)
    kv = mem @ w_kv
    k, v = jnp.split(kv, 2, axis=-1)
    k = k.reshape(b, m, XA_HEADS, XA_DIM).astype(jnp.float32)
    v = v.reshape(b, m, XA_HEADS, XA_DIM).astype(jnp.float32)
    s = jnp.einsum('bthd,bmhd->bhtm', q, k) * (XA_DIM ** -0.5)
    p = jax.nn.softmax(s, axis=-1)
    o = jnp.einsum('bhtm,bmhd->bthd', p, v).reshape(b, t, d).astype(x.dtype)
    return o @ w_o + b_o


def conv_ffn(x, w_up, b_up, w_dw, b_dw, w_down, b_down):
    h = x @ w_up + b_up
    h = lax.conv_general_dilated(h, w_dw[:, None, :], window_strides=(1,), padding='SAME',
                                 dimension_numbers=('NWC', 'WIO', 'NWC'),
                                 feature_group_count=2 * D_FF) + b_dw
    g, u = jnp.split(h, 2, axis=-1)
    y = jax.nn.gelu(g, approximate=False) * u
    return y @ w_down + b_down


def setup_inputs(seed: int = 0) -> dict:
    key = jax.random.key(seed)
    ks = jax.random.split(key, 28)
    L = DEPTH
    nrm = lambda k, shape: jax.random.normal(k, shape, jnp.float32)
    fb = jnp.linspace(3.0, 6.0, ML_HEADS, dtype=jnp.float32)
    f_fwd = 4 * ML_WIDTH + ML_HEADS
    f_bwd = 4 * ML_WIDTH + 3 * ML_HEADS
    b_in = 0.02 * nrm(ks[5], (L, IN_COLS))
    b_in = b_in.at[:, f_fwd:f_fwd + ML_HEADS].add(fb).at[:, f_bwd:f_bwd + ML_HEADS].add(fb)
    return {
        'x': nrm(ks[0], (BATCH, SEQ, D_MODEL)),
        'mem': nrm(ks[1], (BATCH, MEM_LEN, D_MODEL)),
        'ln_in_w': 1.0 + 0.02 * nrm(ks[2], (D_MODEL,)),
        'ln_in_b': 0.02 * nrm(ks[3], (D_MODEL,)),
        'w_in': nrm(ks[4], (L, D_MODEL, IN_COLS)) * D_MODEL ** -0.5,
        'b_in': b_in,
        'ml_norm_w': 1.0 + 0.02 * nrm(ks[6], (L, ML_WIDTH)),
        'na_rpb': 0.02 * nrm(ks[7], (L, NA_HEADS, 2 * NA_ROWS - 1, 2 * NA_COLS - 1)),
        'w_mix_out': nrm(ks[8], (L, D_MODEL, D_MODEL)) * (D_MODEL ** -0.5 * DN_BETA),
        'b_mix_out': 0.02 * nrm(ks[9], (L, D_MODEL)),
        'ln1_w': 1.0 + 0.02 * nrm(ks[10], (L, D_MODEL)),
        'ln1_b': 0.02 * nrm(ks[11], (L, D_MODEL)),
        'w_xq': nrm(ks[12], (L, D_MODEL, D_MODEL)) * D_MODEL ** -0.5,
        'w_xkv': nrm(ks[13], (L, D_MODEL, 2 * D_MODEL)) * D_MODEL ** -0.5,
        'w_xo': nrm(ks[14], (L, D_MODEL, D_MODEL)) * (D_MODEL ** -0.5 * DN_BETA),
        'b_xo': 0.02 * nrm(ks[15], (L, D_MODEL)),
        'ln2_w': 1.0 + 0.02 * nrm(ks[16], (L, D_MODEL)),
        'ln2_b': 0.02 * nrm(ks[17], (L, D_MODEL)),
        'w_up': nrm(ks[18], (L, D_MODEL, 2 * D_FF)) * D_MODEL ** -0.5,
        'b_up': 0.02 * nrm(ks[19], (L, 2 * D_FF)),
        'w_dw': nrm(ks[20], (L, 3, 2 * D_FF)) * 3.0 ** -0.5,
        'b_dw': 0.02 * nrm(ks[21], (L, 2 * D_FF)),
        'w_down': nrm(ks[22], (L, D_FF, D_MODEL)) * (D_FF ** -0.5 * DN_BETA),
        'b_down': 0.02 * nrm(ks[23], (L, D_MODEL)),
        'ln3_w': 1.0 + 0.02 * nrm(ks[24], (L, D_MODEL)),
        'ln3_b': 0.02 * nrm(ks[25], (L, D_MODEL)),
    }


def reference(x, mem, ln_in_w, ln_in_b, w_in, b_in, ml_norm_w, na_rpb, w_mix_out, b_mix_out,
              ln1_w, ln1_b, w_xq, w_xkv, w_xo, b_xo, ln2_w, ln2_b,
              w_up, b_up, w_dw, b_dw, w_down, b_down, ln3_w, ln3_b):
    h = layer_norm(x, ln_in_w, ln_in_b)
    for l in range(DEPTH):
        h = layer_norm(DN_ALPHA * h + hybrid_mixer(h, w_in[l], b_in[l], ml_norm_w[l], na_rpb[l],
                                                    w_mix_out[l], b_mix_out[l]), ln1_w[l], ln1_b[l])
        h = layer_norm(DN_ALPHA * h + memory_xattn(h, mem, w_xq[l], w_xkv[l], w_xo[l], b_xo[l]),
                       ln2_w[l], ln2_b[l])
        h = layer_norm(DN_ALPHA * h + conv_ffn(h, w_up[l], b_up[l], w_dw[l], b_dw[l], w_down[l], b_down[l]),
                       ln3_w[l], ln3_b[l])
    return h
```

```python
import functools

import jax
import jax.numpy as jnp
from jax import lax
from jax.experimental import pallas as pl
from jax.experimental.pallas import tpu as pltpu

F32 = jnp.float32
BF16 = jnp.bfloat16

LN_EPS = 1e-5
GRID_W = 64
ML_HEADS = 4
ML_CHUNK = 128
NA_HEADS = 8
NA_ROWS = 8
NA_COLS = 16
XA_HEADS = 4

VMEM_LIMIT_BYTES = 56 * 1024 * 1024

_NT = (((1,), (1,)), ((), ()))
_TN = (((0,), (0,)), ((), ()))


def _params(*sem):
    return pltpu.CompilerParams(dimension_semantics=sem, vmem_limit_bytes=VMEM_LIMIT_BYTES)


def _layer_norm(x, w, b):
    mu = jnp.mean(x, axis=-1, keepdims=True)
    xc = x - mu
    var = jnp.mean(xc * xc, axis=-1, keepdims=True)
    return xc * lax.rsqrt(var + LN_EPS) * w + b


def _ln_kernel(x_ref, w_ref, b_ref, o_ref):
    o_ref[...] = _layer_norm(x_ref[...], w_ref[...], b_ref[...])


def _input_ln(x, w, b, tm):
    m, d = x.shape
    return pl.pallas_call(
        _ln_kernel,
        grid=(m // tm,),
        in_specs=[pl.BlockSpec((tm, d), lambda i: (i, 0)),
                  pl.BlockSpec((1, d), lambda i: (0, 0)),
                  pl.BlockSpec((1, d), lambda i: (0, 0))],
        out_specs=pl.BlockSpec((tm, d), lambda i: (i, 0)),
        out_shape=jax.ShapeDtypeStruct((m, d), F32),
        compiler_params=_params("parallel"),
        name="input_ln",
    )(x, w, b)


def _inproj_kernel(h_ref, w_ref, b_ref, wg_ref, bg_ref,
                   mq_ref, mk_ref, mv_ref, mo_ref, nq_ref, nk_ref, nv_ref, g_ref, *, width):
    hb = h_ref[...].astype(BF16)
    outs = (mq_ref, mk_ref, mv_ref, mo_ref, nq_ref, nk_ref, nv_ref)
    for j, o_ref in enumerate(outs):
        cols = slice(j * width, (j + 1) * width)
        acc = jnp.dot(hb, w_ref[:, cols], preferred_element_type=F32) + b_ref[:, cols]
        o_ref[...] = acc.astype(o_ref.dtype)
    g_ref[...] = lax.dot_general(wg_ref[...], hb, _NT, preferred_element_type=F32) + bg_ref[...]


def _inproj(h, w_main, b_main, wg_t, bg, tm, width):
    m, d = h.shape
    ncol = w_main.shape[1]
    ng = wg_t.shape[0]
    row = lambda i: (i, 0)
    const = lambda i: (0, 0)
    seg = lambda dt: jax.ShapeDtypeStruct((m, width), dt)
    seg_spec = pl.BlockSpec((tm, width), row)
    return pl.pallas_call(
        functools.partial(_inproj_kernel, width=width),
        grid=(m // tm,),
        in_specs=[pl.BlockSpec((tm, d), row),
                  pl.BlockSpec((d, ncol), const),
                  pl.BlockSpec((1, ncol), const),
                  pl.BlockSpec((ng, d), const),
                  pl.BlockSpec((ng, 1), const)],
        out_specs=[seg_spec] * 7 + [pl.BlockSpec((ng, tm), lambda i: (0, i))],
        out_shape=[seg(BF16), seg(BF16), seg(BF16), seg(F32), seg(BF16), seg(BF16), seg(BF16),
                   jax.ShapeDtypeStruct((ng, m), F32)],
        compiler_params=_params("parallel"),
        name="mixer_inproj",
    )(h, w_main, b_main, wg_t, bg)


def _cumsum_lanes(x, reverse):
    n = x.shape[-1]
    lane = lax.broadcasted_iota(jnp.int32, x.shape, 1)
    k = 1
    while k < n:
        if reverse:
            x = x + jnp.where(lane < n - k, pltpu.roll(x, n - k, axis=1), 0.0)
        else:
            x = x + jnp.where(lane >= k, pltpu.roll(x, k, axis=1), 0.0)
        k *= 2
    return x


def _mlstm_kernel(qf_ref, kf_ref, vf_ref, gf_ref, qb_ref, kb_ref, vb_ref, gb_ref,
                  hf_ref, hb_ref, c_ref, n_ref, m_ref, *, heads, dh):
    L = ML_CHUNK
    scale = dh ** -0.5

    @pl.when(pl.program_id(1) == 0)
    def _():
        c_ref[...] = jnp.zeros_like(c_ref)
        n_ref[...] = jnp.zeros_like(n_ref)
        m_ref[...] = jnp.zeros_like(m_ref)

    t_idx = lax.broadcasted_iota(jnp.int32, (L, L), 0)
    s_idx = lax.broadcasted_iota(jnp.int32, (L, L), 1)
    eye = t_idx == s_idx

    def to_col(x_row):
        return jnp.sum(jnp.where(eye, x_row, 0.0), axis=1, keepdims=True)

    for d, (q_ref, k_ref, v_ref, g_ref, o_ref) in enumerate(
            ((qf_ref, kf_ref, vf_ref, gf_ref, hf_ref), (qb_ref, kb_ref, vb_ref, gb_ref, hb_ref))):
        reverse = d == 1
        g = g_ref[...]
        lf_all = jax.nn.log_sigmoid(g)
        a_all = _cumsum_lanes(lf_all, reverse)
        causal = (s_idx >= t_idx) if reverse else (s_idx <= t_idx)
        for hd in range(heads):
            st = d * heads + hd
            cols = slice(hd * dh, (hd + 1) * dh)
            gi = 2 * d * heads + hd
            gfi = gi + heads
            ig = g[gi:gi + 1, :]
            lf = lf_all[gfi:gfi + 1, :]
            a_row = a_all[gfi:gfi + 1, :]
            q = q_ref[:, cols]
            k = k_ref[:, cols]
            v = v_ref[:, cols]
            c_mat = c_ref[st]
            n_vec = n_ref[st]
            m_prev = m_ref[st][:, 0:1]

            a_col = to_col(a_row)
            d_log = jnp.where(causal, a_col - a_row + ig, -jnp.inf)
            inter_log = a_col + m_prev
            m_t = jnp.maximum(inter_log, jnp.max(d_log, axis=1, keepdims=True))
            s = lax.dot_general(q, k, _NT, preferred_element_type=F32) * scale
            w_qk = jnp.exp(d_log - m_t) * s
            inter = jnp.exp(inter_log - m_t)
            q_c = jnp.dot(q, c_mat.astype(BF16), preferred_element_type=F32)
            num = jnp.dot(w_qk.astype(BF16), v, preferred_element_type=F32) + inter * q_c
            q_n = jnp.sum(q.astype(F32) * n_vec, axis=1, keepdims=True)
            den = jnp.sum(w_qk, axis=1, keepdims=True) + inter * q_n
            o_ref[:, cols] = num / jnp.maximum(jnp.abs(den), jnp.exp(-m_t))

            a_end = jnp.sum(lf, axis=1, keepdims=True)
            up_log = a_end - a_row + ig
            m_new = jnp.maximum(a_end + m_prev, jnp.max(up_log, axis=1, keepdims=True))
            kw = k.astype(F32) * (to_col(jnp.exp(up_log - m_new)) * scale)
            decay = jnp.exp(a_end + m_prev - m_new)
            c_ref[st] = decay * c_mat + lax.dot_general(kw.astype(BF16), v, _TN,
                                                        preferred_element_type=F32)
            n_ref[st] = decay * n_vec + jnp.sum(kw, axis=0, keepdims=True)
            m_ref[st] = jnp.broadcast_to(m_new, (1, 128))


def _mlstm(mq, mk, mv, gates, batch, heads):
    m, width = mq.shape
    dh = width // heads
    nc = m // batch // ML_CHUNK
    ng = gates.shape[0]
    fwd = lambda b, c: (b * nc + c, 0)
    bwd = lambda b, c: (b * nc + nc - 1 - c, 0)
    fwd_g = lambda b, c: (0, b * nc + c)
    bwd_g = lambda b, c: (0, b * nc + nc - 1 - c)
    blk = lambda im: pl.BlockSpec((ML_CHUNK, width), im)
    gblk = lambda im: pl.BlockSpec((ng, ML_CHUNK), im)
    return pl.pallas_call(
        functools.partial(_mlstm_kernel, heads=heads, dh=dh),
        grid=(batch, nc),
        in_specs=[blk(fwd), blk(fwd), blk(fwd), gblk(fwd_g),
                  blk(bwd), blk(bwd), blk(bwd), gblk(bwd_g)],
        out_specs=[blk(fwd), blk(bwd)],
        out_shape=[jax.ShapeDtypeStruct((m, width), F32)] * 2,
        scratch_shapes=[pltpu.VMEM((2 * heads, dh, dh), F32),
                        pltpu.VMEM((2 * heads, 1, dh), F32),
                        pltpu.VMEM((2 * heads, 1, 128), F32)],
        compiler_params=_params("parallel", "arbitrary"),
        name="mlstm_scan",
    )(mq, mk, mv, gates, mq, mk, mv, gates)


def _na_bias_table(rpb):
    j = jnp.arange(GRID_W)
    dc = jnp.clip(j[None, :] - j[:, None] + NA_COLS - 1, 0, 2 * NA_COLS - 2)
    cs = jnp.clip(j - NA_COLS // 2, 0, GRID_W - NA_COLS)
    ok = (j[None, :] >= cs[:, None]) & (j[None, :] < cs[:, None] + NA_COLS)
    dr = jnp.arange(NA_ROWS)[:, None] + jnp.arange(NA_ROWS)[None, :]
    tab = rpb.astype(F32)[:, dr][:, :, :, dc]
    tab = jnp.where(ok[None, None, None], tab, -jnp.inf)
    tab = tab.transpose(1, 0, 3, 2, 4)
    return tab.reshape(NA_ROWS, rpb.shape[0], GRID_W, NA_ROWS * GRID_W)


def _na_window_start(r, rows):
    return jnp.clip(r - NA_ROWS // 2, 0, rows - NA_ROWS)


def _na_kernel(q_ref, k_ref, v_ref, bias_ref, o_ref, *, rows, heads, dh):
    r = pl.program_id(1)
    tok0 = pl.multiple_of(_na_window_start(r, rows) * GRID_W, GRID_W)
    nkeys = NA_ROWS * GRID_W
    scale = dh ** -0.5
    for hd in range(heads):
        cols = slice(hd * dh, (hd + 1) * dh)
        q = q_ref[:, cols]
        k = k_ref[pl.ds(tok0, nkeys), cols]
        v = v_ref[pl.ds(tok0, nkeys), cols]
        s = lax.dot_general(q, k, _NT, preferred_element_type=F32) * scale + bias_ref[0, hd]
        p = jnp.exp(s - jnp.max(s, axis=1, keepdims=True))
        denom = jnp.sum(p, axis=1, keepdims=True)
        o = jnp.dot(p.astype(BF16), v, preferred_element_type=F32) / denom
        o_ref[:, cols] = o.astype(o_ref.dtype)


def _na(nq, nk, nv, bias_tab, batch, heads):
    m, width = nq.shape
    t = m // batch
    rows = t // GRID_W
    dh = width // heads
    nkeys = NA_ROWS * GRID_W

    def bias_map(b, r):
        off = NA_ROWS - 1 - (r - _na_window_start(r, rows))
        return (off, 0, 0, 0)

    return pl.pallas_call(
        functools.partial(_na_kernel, rows=rows, heads=heads, dh=dh),
        grid=(batch, rows),
        in_specs=[pl.BlockSpec((GRID_W, width), lambda b, r: (b * rows + r, 0)),
                  pl.BlockSpec((t, width), lambda b, r: (b, 0)),
                  pl.BlockSpec((t, width), lambda b, r: (b, 0)),
                  pl.BlockSpec((1, heads, GRID_W, nkeys), bias_map)],
        out_specs=pl.BlockSpec((GRID_W, width), lambda b, r: (b * rows + r, 0)),
        out_shape=jax.ShapeDtypeStruct((m, width), BF16),
        compiler_params=_params("parallel", "arbitrary"),
        name="neighborhood_attn",
    )(nq, nk, nv, bias_tab)


def _outproj_kernel(h_ref, hf_ref, hb_ref, mo_ref, na_ref, nw_ref, w_ref, b_ref, lw_ref, lb_ref,
                    o_ref, *, heads, alpha):
    width = hf_ref.shape[1]
    dh = width // heads
    hm = hf_ref[...] + hb_ref[...]
    parts = []
    for hd in range(heads):
        x = hm[:, hd * dh:(hd + 1) * dh]
        mu = jnp.mean(x, axis=-1, keepdims=True)
        xc = x - mu
        var = jnp.mean(xc * xc, axis=-1, keepdims=True)
        parts.append(xc * lax.rsqrt(var + LN_EPS))
    hn = jnp.concatenate(parts, axis=1)
    ml = jax.nn.sigmoid(mo_ref[...]) * (hn * nw_ref[...])
    y = (jnp.dot(ml.astype(BF16), w_ref[0:width, :], preferred_element_type=F32)
         + jnp.dot(na_ref[...], w_ref[width:, :], preferred_element_type=F32) + b_ref[...])
    o_ref[...] = _layer_norm(alpha * h_ref[...] + y, lw_ref[...], lb_ref[...])


def _outproj(h, hf, hb, mo, na, nw, w, b, lw, lb, tm, heads, alpha):
    m, d = h.shape
    width = hf.shape[1]
    row = lambda i: (i, 0)
    const = lambda i: (0, 0)
    return pl.pallas_call(
        functools.partial(_outproj_kernel, heads=heads, alpha=alpha),
        grid=(m // tm,),
        in_specs=[pl.BlockSpec((tm, d), row)] + [pl.BlockSpec((tm, width), row)] * 4
                 + [pl.BlockSpec((1, width), const), pl.BlockSpec((d, d), const)]
                 + [pl.BlockSpec((1, d), const)] * 3,
        out_specs=pl.BlockSpec((tm, d), row),
        out_shape=jax.ShapeDtypeStruct((m, d), F32),
        compiler_params=_params("parallel"),
        name="mixer_outproj_ln",
    )(h, hf, hb, mo, na, nw, w, b, lw, lb)


def _kvproj_kernel(mem_ref, w_ref, o_ref):
    o_ref[0] = jnp.dot(mem_ref[...].astype(BF16), w_ref[0],
                       preferred_element_type=F32).astype(o_ref.dtype)


def _kvproj(mem, w_kv):
    mm, d = mem.shape
    depth, _, n = w_kv.shape
    return pl.pallas_call(
        _kvproj_kernel,
        grid=(depth,),
        in_specs=[pl.BlockSpec((mm, d), lambda l: (0, 0)),
                  pl.BlockSpec((1, d, n), lambda l: (l, 0, 0))],
        out_specs=pl.BlockSpec((1, mm, n), lambda l: (l, 0, 0)),
        out_shape=jax.ShapeDtypeStruct((depth, mm, n), BF16),
        compiler_params=_params("parallel"),
        name="memory_kv_proj",
    )(mem, w_kv)


def _xattn_kernel(h_ref, kv_ref, wq_ref, wo_ref, bo_ref, lw_ref, lb_ref, o_ref, *, heads, alpha):
    d = h_ref.shape[1]
    dh = d // heads
    scale = dh ** -0.5
    h = h_ref[...]
    q = jnp.dot(h.astype(BF16), wq_ref[...], preferred_element_type=F32).astype(BF16)
    outs = []
    for hd in range(heads):
        qh = q[:, hd * dh:(hd + 1) * dh]
        kh = kv_ref[:, hd * dh:(hd + 1) * dh]
        vh = kv_ref[:, d + hd * dh:d + (hd + 1) * dh]
        s = lax.dot_general(qh, kh, _NT, preferred_element_type=F32) * scale
        p = jnp.exp(s - jnp.max(s, axis=1, keepdims=True))
        denom = jnp.sum(p, axis=1, keepdims=True)
        o = jnp.dot(p.astype(BF16), vh, preferred_element_type=F32) / denom
        outs.append(o.astype(BF16))
    o_all = jnp.concatenate(outs, axis=1)
    y = jnp.dot(o_all, wo_ref[...], preferred_element_type=F32) + bo_ref[...]
    o_ref[...] = _layer_norm(alpha * h + y, lw_ref[...], lb_ref[...])


def _xattn(h, kv, wq, wo, bo, lw, lb, batch, tm, heads, alpha):
    m, d = h.shape
    nt = m // batch // tm
    mem_len = kv.shape[0] // batch
    row = lambda b, i: (b * nt + i, 0)
    const = lambda b, i: (0, 0)
    return pl.pallas_call(
        functools.partial(_xattn_kernel, heads=heads, alpha=alpha),
        grid=(batch, nt),
        in_specs=[pl.BlockSpec((tm, d), row),
                  pl.BlockSpec((mem_len, 2 * d), lambda b, i: (b, 0)),
                  pl.BlockSpec((d, d), const), pl.BlockSpec((d, d), const)]
                 + [pl.BlockSpec((1, d), const)] * 3,
        out_specs=pl.BlockSpec((tm, d), row),
        out_shape=jax.ShapeDtypeStruct((m, d), F32),
        compiler_params=_params("parallel", "parallel"),
        name="memory_xattn_ln",
    )(h, kv, wq, wo, bo, lw, lb)


FFN_HALO = 16


def _ffn_kernel(h_ref, hp_ref, hn_ref, wg_ref, wu_ref, bg_ref, bu_ref, dwg_ref, dwu_ref,
                cbg_ref, cbu_ref, wd_ref, bd_ref, lw_ref, lb_ref, o_ref, hb_ref, acc_ref, *, alpha):
    i = pl.program_id(1)
    j = pl.program_id(2)
    tm = h_ref.shape[0]
    hal = FFN_HALO

    @pl.when(j == 0)
    def _():
        hb_ref[0:hal, :] = hp_ref[...].astype(BF16)
        hb_ref[hal:hal + tm, :] = h_ref[...].astype(BF16)
        hb_ref[hal + tm:, :] = hn_ref[...].astype(BF16)
        acc_ref[...] = jnp.zeros_like(acc_ref)

    ridx = lax.broadcasted_iota(jnp.int32, (tm + 2 * hal, 1), 0)
    inside = jnp.logical_and(jnp.logical_or(ridx >= hal, i > 0),
                             jnp.logical_or(ridx < hal + tm, i < pl.num_programs(1) - 1))
    hb = hb_ref[...]

    def branch(w_ref, b_ref, dw_ref, cb_ref):
        up = jnp.dot(hb, w_ref[...], preferred_element_type=F32) + b_ref[...]
        up = jnp.where(inside, up, 0.0)
        return (dw_ref[0:1, :] * up[hal - 1:hal - 1 + tm] + dw_ref[1:2, :] * up[hal:hal + tm]
                + dw_ref[2:3, :] * up[hal + 1:hal + 1 + tm] + cb_ref[...])

    g = branch(wg_ref, bg_ref, dwg_ref, cbg_ref)
    u = branch(wu_ref, bu_ref, dwu_ref, cbu_ref)
    y = (0.5 * g * (1.0 + lax.erf(g * (2.0 ** -0.5)))) * u
    acc_ref[...] += jnp.dot(y.astype(BF16), wd_ref[...], preferred_element_type=F32)

    @pl.when(j == pl.num_programs(2) - 1)
    def _():
        o_ref[...] = _layer_norm(alpha * h_ref[...] + acc_ref[...] + bd_ref[...],
                                 lw_ref[...], lb_ref[...])


def _ffn(h, w_up, b_up, w_dw, b_dw, w_down, b_down, lw, lb, batch, tm, tf, alpha):
    m, d = h.shape
    dff = w_down.shape[0]
    t = m // batch
    nt = t // tm
    nj = dff // tf
    hal = FFN_HALO
    row = lambda b, i, j: (b * nt + i, 0)
    prev = lambda b, i, j: (jnp.maximum((b * t + i * tm) // hal - 1, 0), 0)
    nxt = lambda b, i, j: (jnp.minimum((b * t + (i + 1) * tm) // hal, m // hal - 1), 0)
    gcol = lambda b, i, j: (0, j)
    ucol = lambda b, i, j: (0, nj + j)
    const = lambda b, i, j: (0, 0)
    return pl.pallas_call(
        functools.partial(_ffn_kernel, alpha=alpha),
        grid=(batch, nt, nj),
        in_specs=[pl.BlockSpec((tm, d), row), pl.BlockSpec((hal, d), prev), pl.BlockSpec((hal, d), nxt),
                  pl.BlockSpec((d, tf), gcol), pl.BlockSpec((d, tf), ucol),
                  pl.BlockSpec((1, tf), gcol), pl.BlockSpec((1, tf), ucol),
                  pl.BlockSpec((3, tf), gcol), pl.BlockSpec((3, tf), ucol),
                  pl.BlockSpec((1, tf), gcol), pl.BlockSpec((1, tf), ucol),
                  pl.BlockSpec((tf, d), lambda b, i, j: (j, 0))]
                 + [pl.BlockSpec((1, d), const)] * 3,
        out_specs=pl.BlockSpec((tm, d), row),
        out_shape=jax.ShapeDtypeStruct((m, d), F32),
        scratch_shapes=[pltpu.VMEM((tm + 2 * hal, d), BF16), pltpu.VMEM((tm, d), F32)],
        compiler_params=_params("parallel", "parallel", "arbitrary"),
        name="conv_ffn_ln",
    )(h, h, h, w_up, w_up, b_up, b_up, w_dw, w_dw, b_dw, b_dw, w_down, b_down, lw, lb)


def _pick(n, prefs):
    for p in prefs:
        if n % p == 0:
            return p
    return n


def kernel(x, mem, ln_in_w, ln_in_b, w_in, b_in, ml_norm_w, na_rpb, w_mix_out, b_mix_out,
           ln1_w, ln1_b, w_xq, w_xkv, w_xo, b_xo, ln2_w, ln2_b,
           w_up, b_up, w_dw, b_dw, w_down, b_down, ln3_w, ln3_b):
    batch, t, d = x.shape
    depth = w_in.shape[0]
    m = batch * t
    ml_width = d // 2
    na_width = d - ml_width
    dff = w_down.shape[1]
    alpha = (2.0 * depth) ** 0.25
    assert ml_width == na_width and t % (GRID_W * NA_ROWS) == 0 and t % ML_CHUNK == 0

    tm = _pick(t, (512, 256, 128))
    tm_ffn = _pick(t, (1024, 512, 256, 128))
    tf = _pick(dff, (256, 128))

    row2 = lambda a: a.reshape(1, -1)
    g0 = 4 * ml_width
    g1 = g0 + 4 * ML_HEADS

    h = _input_ln(x.reshape(m, d), row2(ln_in_w), row2(ln_in_b), tm)
    kv_all = _kvproj(mem.reshape(-1, d), w_xkv.astype(BF16))

    for l in range(depth):
        w_main = jnp.concatenate([w_in[l, :, :g0], w_in[l, :, g1:]], axis=1).astype(BF16)
        b_main = row2(jnp.concatenate([b_in[l, :g0], b_in[l, g1:]]))
        wg_t = w_in[l, :, g0:g1].T.astype(BF16)
        bg = b_in[l, g0:g1].reshape(-1, 1)
        mq, mk, mv, mo, nq, nk, nv, gates = _inproj(h, w_main, b_main, wg_t, bg, tm, ml_width)
        hf, hb = _mlstm(mq, mk, mv, gates, batch, ML_HEADS)
        na = _na(nq, nk, nv, _na_bias_table(na_rpb[l]), batch, NA_HEADS)
        h = _outproj(h, hf, hb, mo, na, row2(ml_norm_w[l]), w_mix_out[l].astype(BF16),
                     row2(b_mix_out[l]), row2(ln1_w[l]), row2(ln1_b[l]), tm, ML_HEADS, alpha)
        h = _xattn(h, kv_all[l], w_xq[l].astype(BF16), w_xo[l].astype(BF16), row2(b_xo[l]),
                   row2(ln2_w[l]), row2(ln2_b[l]), batch, tm, XA_HEADS, alpha)
        h = _ffn(h, w_up[l].astype(BF16), row2(b_up[l]), w_dw[l], row2(b_dw[l]),
                 w_down[l].astype(BF16), row2(b_down[l]), row2(ln3_w[l]), row2(ln3_b[l]),
                 batch, tm_ffn, tf, alpha)
    return h.reshape(batch, t, d)
```

```python
import functools

import jax
import jax.numpy as jnp
from jax import lax
from jax.experimental import pallas as pl
from jax.experimental.pallas import tpu as pltpu

F32 = jnp.float32
BF16 = jnp.bfloat16

LN_EPS = 1e-5
GRID_W = 64
ML_HEADS = 4
ML_CHUNK = 128
NA_HEADS = 8
NA_ROWS = 8
NA_COLS = 16
XA_HEADS = 4

VMEM_LIMIT_BYTES = 56 * 1024 * 1024

_NT = (((1,), (1,)), ((), ()))
_TN = (((0,), (0,)), ((), ()))


def _params(*sem):
    return pltpu.CompilerParams(dimension_semantics=sem, vmem_limit_bytes=VMEM_LIMIT_BYTES)


def _layer_norm(x, w, b):
    mu = jnp.mean(x, axis=-1, keepdims=True)
    xc = x - mu
    var = jnp.mean(xc * xc, axis=-1, keepdims=True)
    return xc * lax.rsqrt(var + LN_EPS) * w + b


def _ln_kernel(x_ref, w_ref, b_ref, o_ref):
    o_ref[...] = _layer_norm(x_ref[...], w_ref[...], b_ref[...])


def _input_ln(x, w, b, tm):
    m, d = x.shape
    return pl.pallas_call(
        _ln_kernel,
        grid=(m // tm,),
        in_specs=[pl.BlockSpec((tm, d), lambda i: (i, 0)),
                  pl.BlockSpec((1, d), lambda i: (0, 0)),
                  pl.BlockSpec((1, d), lambda i: (0, 0))],
        out_specs=pl.BlockSpec((tm, d), lambda i: (i, 0)),
        out_shape=jax.ShapeDtypeStruct((m, d), F32),
        compiler_params=_params("parallel"),
        name="input_ln",
    )(x, w, b)


def _inproj_kernel(h_ref, w_ref, b_ref, wk_ref, bk_ref, wg_ref, bg_ref,
                   mq_ref, mv_ref, mo_ref, nq_ref, nk_ref, nv_ref, kt_ref, g_ref, *, width):
    hb = h_ref[...].astype(BF16)
    outs = (mq_ref, mv_ref, mo_ref, nq_ref, nk_ref, nv_ref)
    for j, o_ref in enumerate(outs):
        cols = slice(j * width, (j + 1) * width)
        acc = jnp.dot(hb, w_ref[:, cols], preferred_element_type=F32) + b_ref[:, cols]
        o_ref[...] = acc.astype(o_ref.dtype)
    kt = lax.dot_general(wk_ref[...], hb, _NT, preferred_element_type=F32) + bk_ref[...]
    kt_ref[...] = kt.astype(kt_ref.dtype)
    g_ref[...] = lax.dot_general(wg_ref[...], hb, _NT, preferred_element_type=F32) + bg_ref[...]


def _inproj(h, w_main, b_main, wk_t, bk, wg_t, bg, tm, width):
    m, d = h.shape
    ncol = w_main.shape[1]
    ng = wg_t.shape[0]
    row = lambda i: (i, 0)
    col = lambda i: (0, i)
    const = lambda i: (0, 0)
    seg = lambda dt: jax.ShapeDtypeStruct((m, width), dt)
    seg_spec = pl.BlockSpec((tm, width), row)
    return pl.pallas_call(
        functools.partial(_inproj_kernel, width=width),
        grid=(m // tm,),
        in_specs=[pl.BlockSpec((tm, d), row),
                  pl.BlockSpec((d, ncol), const),
                  pl.BlockSpec((1, ncol), const),
                  pl.BlockSpec((width, d), const),
                  pl.BlockSpec((width, 1), const),
                  pl.BlockSpec((ng, d), const),
                  pl.BlockSpec((ng, 1), const)],
        out_specs=[seg_spec] * 6 + [pl.BlockSpec((width, tm), col), pl.BlockSpec((ng, tm), col)],
        out_shape=[seg(BF16), seg(BF16), seg(F32), seg(BF16), seg(BF16), seg(BF16),
                   jax.ShapeDtypeStruct((width, m), BF16), jax.ShapeDtypeStruct((ng, m), F32)],
        compiler_params=_params("parallel"),
        name="mixer_inproj",
    )(h, w_main, b_main, wk_t, bk, wg_t, bg)


def _cumsum_lanes(x, reverse):
    n = x.shape[-1]
    lane = lax.broadcasted_iota(jnp.int32, x.shape, 1)
    k = 1
    while k < n:
        if reverse:
            x = x + jnp.where(lane < n - k, pltpu.roll(x, n - k, axis=1), 0.0)
        else:
            x = x + jnp.where(lane >= k, pltpu.roll(x, k, axis=1), 0.0)
        k *= 2
    return x


def _mlstm_kernel(qf_ref, ktf_ref, vf_ref, gf_ref, qb_ref, ktb_ref, vb_ref, gb_ref,
                  hf_ref, hb_ref, c_ref, m_ref, *, heads, dh):
    L = ML_CHUNK
    scale = dh ** -0.5

    @pl.when(pl.program_id(1) == 0)
    def _():
        c_ref[...] = jnp.zeros_like(c_ref)
        m_ref[...] = jnp.zeros_like(m_ref)

    t_idx = lax.broadcasted_iota(jnp.int32, (L, L), 0)
    s_idx = lax.broadcasted_iota(jnp.int32, (L, L), 1)
    eye = t_idx == s_idx

    def to_col(x_row):
        return jnp.sum(jnp.where(eye, x_row, 0.0), axis=1, keepdims=True)

    ones = jnp.ones((L, dh), BF16)
    probs = []
    for d, (q_ref, kt_ref, v_ref, g_ref, o_ref) in enumerate(
            ((qf_ref, ktf_ref, vf_ref, gf_ref, hf_ref), (qb_ref, ktb_ref, vb_ref, gb_ref, hb_ref))):
        reverse = d == 1
        g = g_ref[...]
        lf_all = jax.nn.log_sigmoid(g)
        a_all = _cumsum_lanes(lf_all, reverse)
        causal = (s_idx >= t_idx) if reverse else (s_idx <= t_idx)
        for hd in range(heads):
            gi = 2 * d * heads + hd
            gfi = gi + heads
            probs.append(dict(
                st=d * heads + hd, cols=slice(hd * dh, (hd + 1) * dh), causal=causal,
                q_ref=q_ref, kt_ref=kt_ref, v_ref=v_ref, o_ref=o_ref,
                ig=g[gi:gi + 1, :], lf=lf_all[gfi:gfi + 1, :], a_row=a_all[gfi:gfi + 1, :]))

    for p in probs:
        p["m_prev"] = m_ref[p["st"]][:, 0:1]
        p["a_col"] = to_col(p["a_row"])
    for p in probs:
        d_log = jnp.where(p["causal"], p["a_col"] - p["a_row"] + p["ig"], -jnp.inf)
        p["inter_log"] = p["a_col"] + p["m_prev"]
        p["m_t"] = jnp.maximum(p["inter_log"], jnp.max(d_log, axis=1, keepdims=True))
        p["d_log"] = d_log
    for p in probs:
        p["q"] = p["q_ref"][:, p["cols"]]
        p["s"] = jnp.dot(p["q"], p["kt_ref"][p["cols"], :], preferred_element_type=F32)
    for p in probs:
        p["w_qk"] = (jnp.exp(p["d_log"] - p["m_t"]) * (p["s"] * scale)).astype(BF16)
        p["inter"] = jnp.exp(p["inter_log"] - p["m_t"])
    for p in probs:
        p["v_aug"] = jnp.concatenate([p["v_ref"][:, p["cols"]], ones], axis=1)
        p["c_aug"] = c_ref[p["st"]]
        q_c = jnp.dot(p["q"], p["c_aug"].astype(BF16), preferred_element_type=F32)
        tot = jnp.dot(p["w_qk"], p["v_aug"], preferred_element_type=F32) + p["inter"] * q_c
        num, den = tot[:, :dh], tot[:, dh:]
        p["o_ref"][:, p["cols"]] = num / jnp.maximum(jnp.abs(den), jnp.exp(-p["m_t"]))
    for p in probs:
        a_end = jnp.sum(p["lf"], axis=1, keepdims=True)
        up_log = a_end - p["a_row"] + p["ig"]
        m_new = jnp.maximum(a_end + p["m_prev"], jnp.max(up_log, axis=1, keepdims=True))
        wk = jnp.exp(up_log - m_new) * scale
        kw_t = (p["kt_ref"][p["cols"], :].astype(F32) * wk).astype(BF16)
        decay = jnp.exp(a_end + p["m_prev"] - m_new)
        c_ref[p["st"]] = decay * p["c_aug"] + jnp.dot(kw_t, p["v_aug"], preferred_element_type=F32)
        m_ref[p["st"]] = jnp.broadcast_to(m_new, (1, 128))


def _mlstm(mq, mk_t, mv, gates, batch, heads):
    m, width = mq.shape
    dh = width // heads
    nc = m // batch // ML_CHUNK
    ng = gates.shape[0]
    fwd = lambda b, c: (b * nc + c, 0)
    bwd = lambda b, c: (b * nc + nc - 1 - c, 0)
    fwd_t = lambda b, c: (0, b * nc + c)
    bwd_t = lambda b, c: (0, b * nc + nc - 1 - c)
    blk = lambda im: pl.BlockSpec((ML_CHUNK, width), im)
    tblk = lambda im: pl.BlockSpec((width, ML_CHUNK), im)
    gblk = lambda im: pl.BlockSpec((ng, ML_CHUNK), im)
    return pl.pallas_call(
        functools.partial(_mlstm_kernel, heads=heads, dh=dh),
        grid=(batch, nc),
        in_specs=[blk(fwd), tblk(fwd_t), blk(fwd), gblk(fwd_t),
                  blk(bwd), tblk(bwd_t), blk(bwd), gblk(bwd_t)],
        out_specs=[blk(fwd), blk(bwd)],
        out_shape=[jax.ShapeDtypeStruct((m, width), F32)] * 2,
        scratch_shapes=[pltpu.VMEM((2 * heads, dh, 2 * dh), F32),
                        pltpu.VMEM((2 * heads, 1, 128), F32)],
        compiler_params=_params("parallel", "arbitrary"),
        name="mlstm_scan",
    )(mq, mk_t, mv, gates, mq, mk_t, mv, gates)


def _na_bias_table(rpb):
    j = jnp.arange(GRID_W)
    dc = jnp.clip(j[None, :] - j[:, None] + NA_COLS - 1, 0, 2 * NA_COLS - 2)
    cs = jnp.clip(j - NA_COLS // 2, 0, GRID_W - NA_COLS)
    ok = (j[None, :] >= cs[:, None]) & (j[None, :] < cs[:, None] + NA_COLS)
    onehot = (dc[..., None] == jnp.arange(2 * NA_COLS - 1)).astype(F32)
    cols = jnp.einsum('hdc,qkc->hdqk', rpb.astype(F32), onehot, precision=lax.Precision.HIGHEST)
    cols = jnp.where(ok[None, None], cols, -jnp.inf)
    tab = jnp.stack([cols[:, off:off + NA_ROWS] for off in range(NA_ROWS)])
    tab = tab.transpose(0, 1, 3, 2, 4)
    return tab.reshape(NA_ROWS, rpb.shape[0], GRID_W, NA_ROWS * GRID_W)


def _na_window_start(r, rows):
    return jnp.clip(r - NA_ROWS // 2, 0, rows - NA_ROWS)


NA_ROWS_PER_STEP = 2


def _na_kernel(q_ref, k_ref, v_ref, bias_ref, o_ref, *, rows, heads, dh):
    nkeys = NA_ROWS * GRID_W
    scale = dh ** -0.5
    cols = [slice(hd * dh, (hd + 1) * dh) for hd in range(heads)]
    probs = []
    for rr in range(NA_ROWS_PER_STEP):
        r = pl.program_id(1) * NA_ROWS_PER_STEP + rr
        start = _na_window_start(r, rows)
        tok0 = pl.multiple_of(start * GRID_W, GRID_W)
        off = NA_ROWS - 1 - (r - start)
        for hd in range(heads):
            probs.append((rr, hd, tok0, off))
    s = [lax.dot_general(q_ref[rr * GRID_W:(rr + 1) * GRID_W, cols[hd]],
                         k_ref[pl.ds(tok0, nkeys), cols[hd]], _NT,
                         preferred_element_type=F32) * scale + bias_ref[off, hd]
         for rr, hd, tok0, off in probs]
    mx = [jnp.max(x, axis=1, keepdims=True) for x in s]
    p = [jnp.exp(x - m) for x, m in zip(s, mx)]
    denom = [jnp.sum(x, axis=1, keepdims=True) for x in p]
    o = [jnp.dot(x.astype(BF16), v_ref[pl.ds(tok0, nkeys), cols[hd]],
                 preferred_element_type=F32) / dn
         for x, dn, (rr, hd, tok0, off) in zip(p, denom, probs)]
    for rr in range(NA_ROWS_PER_STEP):
        o_ref[rr * GRID_W:(rr + 1) * GRID_W, :] = jnp.concatenate(
            o[rr * heads:(rr + 1) * heads], axis=1).astype(o_ref.dtype)


def _na(nq, nk, nv, bias_tab, batch, heads):
    m, width = nq.shape
    t = m // batch
    rows = t // GRID_W
    dh = width // heads
    nkeys = NA_ROWS * GRID_W
    rps = NA_ROWS_PER_STEP
    steps = rows // rps
    qblk = pl.BlockSpec((rps * GRID_W, width), lambda b, r: (b * steps + r, 0))
    return pl.pallas_call(
        functools.partial(_na_kernel, rows=rows, heads=heads, dh=dh),
        grid=(batch, steps),
        in_specs=[qblk,
                  pl.BlockSpec((t, width), lambda b, r: (b, 0)),
                  pl.BlockSpec((t, width), lambda b, r: (b, 0)),
                  pl.BlockSpec((NA_ROWS, heads, GRID_W, nkeys), lambda b, r: (0, 0, 0, 0))],
        out_specs=qblk,
        out_shape=jax.ShapeDtypeStruct((m, width), BF16),
        compiler_params=_params("parallel", "arbitrary"),
        name="neighborhood_attn",
    )(nq, nk, nv, bias_tab)


def _outproj_kernel(h_ref, hf_ref, hb_ref, mo_ref, na_ref, nw_ref, w_ref, b_ref, lw_ref, lb_ref,
                    o_ref, *, heads, alpha):
    width = hf_ref.shape[1]
    dh = width // heads
    hm = hf_ref[...] + hb_ref[...]
    parts = []
    for hd in range(heads):
        x = hm[:, hd * dh:(hd + 1) * dh]
        mu = jnp.mean(x, axis=-1, keepdims=True)
        xc = x - mu
        var = jnp.mean(xc * xc, axis=-1, keepdims=True)
        parts.append(xc * lax.rsqrt(var + LN_EPS))
    hn = jnp.concatenate(parts, axis=1)
    ml = jax.nn.sigmoid(mo_ref[...]) * (hn * nw_ref[...])
    y = (jnp.dot(ml.astype(BF16), w_ref[0:width, :], preferred_element_type=F32)
         + jnp.dot(na_ref[...], w_ref[width:, :], preferred_element_type=F32) + b_ref[...])
    o_ref[...] = _layer_norm(alpha * h_ref[...] + y, lw_ref[...], lb_ref[...])


def _outproj(h, hf, hb, mo, na, nw, w, b, lw, lb, tm, heads, alpha):
    m, d = h.shape
    width = hf.shape[1]
    row = lambda i: (i, 0)
    const = lambda i: (0, 0)
    return pl.pallas_call(
        functools.partial(_outproj_kernel, heads=heads, alpha=alpha),
        grid=(m // tm,),
        in_specs=[pl.BlockSpec((tm, d), row)] + [pl.BlockSpec((tm, width), row)] * 4
                 + [pl.BlockSpec((1, width), const), pl.BlockSpec((d, d), const)]
                 + [pl.BlockSpec((1, d), const)] * 3,
        out_specs=pl.BlockSpec((tm, d), row),
        out_shape=jax.ShapeDtypeStruct((m, d), F32),
        compiler_params=_params("parallel"),
        name="mixer_outproj_ln",
    )(h, hf, hb, mo, na, nw, w, b, lw, lb)


def _kvproj_kernel(mem_ref, w_ref, o_ref):
    o_ref[0] = jnp.dot(mem_ref[...].astype(BF16), w_ref[0],
                       preferred_element_type=F32).astype(o_ref.dtype)


def _kvproj(mem, w_kv):
    mm, d = mem.shape
    depth, _, n = w_kv.shape
    return pl.pallas_call(
        _kvproj_kernel,
        grid=(depth,),
        in_specs=[pl.BlockSpec((mm, d), lambda l: (0, 0)),
                  pl.BlockSpec((1, d, n), lambda l: (l, 0, 0))],
        out_specs=pl.BlockSpec((1, mm, n), lambda l: (l, 0, 0)),
        out_shape=jax.ShapeDtypeStruct((depth, mm, n), BF16),
        compiler_params=_params("parallel"),
        name="memory_kv_proj",
    )(mem, w_kv)


def _xattn_kernel(h_ref, kv_ref, wq_ref, wo_ref, bo_ref, lw_ref, lb_ref, o_ref, *, heads, alpha):
    d = h_ref.shape[1]
    dh = d // heads
    scale = dh ** -0.5
    h = h_ref[...]
    q = jnp.dot(h.astype(BF16), wq_ref[...], preferred_element_type=F32).astype(BF16)
    outs = []
    for hd in range(heads):
        qh = q[:, hd * dh:(hd + 1) * dh]
        kh = kv_ref[:, hd * dh:(hd + 1) * dh]
        vh = kv_ref[:, d + hd * dh:d + (hd + 1) * dh]
        s = lax.dot_general(qh, kh, _NT, preferred_element_type=F32) * scale
        p = jnp.exp(s - jnp.max(s, axis=1, keepdims=True))
        denom = jnp.sum(p, axis=1, keepdims=True)
        o = jnp.dot(p.astype(BF16), vh, preferred_element_type=F32) / denom
        outs.append(o.astype(BF16))
    o_all = jnp.concatenate(outs, axis=1)
    y = jnp.dot(o_all, wo_ref[...], preferred_element_type=F32) + bo_ref[...]
    o_ref[...] = _layer_norm(alpha * h + y, lw_ref[...], lb_ref[...])


def _xattn(h, kv, wq, wo, bo, lw, lb, batch, tm, heads, alpha):
    m, d = h.shape
    nt = m // batch // tm
    mem_len = kv.shape[0] // batch
    row = lambda b, i: (b * nt + i, 0)
    const = lambda b, i: (0, 0)
    return pl.pallas_call(
        functools.partial(_xattn_kernel, heads=heads, alpha=alpha),
        grid=(batch, nt),
        in_specs=[pl.BlockSpec((tm, d), row),
                  pl.BlockSpec((mem_len, 2 * d), lambda b, i: (b, 0)),
                  pl.BlockSpec((d, d), const), pl.BlockSpec((d, d), const)]
                 + [pl.BlockSpec((1, d), const)] * 3,
        out_specs=pl.BlockSpec((tm, d), row),
        out_shape=jax.ShapeDtypeStruct((m, d), F32),
        compiler_params=_params("parallel", "parallel"),
        name="memory_xattn_ln",
    )(h, kv, wq, wo, bo, lw, lb)


FFN_HALO = 16


def _ffn_kernel(h_ref, hp_ref, hn_ref, wg_ref, wu_ref, bg_ref, bu_ref, dwg_ref, dwu_ref,
                cbg_ref, cbu_ref, wd_ref, bd_ref, lw_ref, lb_ref, o_ref, hb_ref, acc_ref, *, alpha):
    i = pl.program_id(1)
    j = pl.program_id(2)
    tm = h_ref.shape[0]
    hal = FFN_HALO

    @pl.when(j == 0)
    def _():
        hb_ref[0:hal, :] = hp_ref[...].astype(BF16)
        hb_ref[hal:hal + tm, :] = h_ref[...].astype(BF16)
        hb_ref[hal + tm:, :] = hn_ref[...].astype(BF16)
        acc_ref[...] = jnp.zeros_like(acc_ref)

    ridx = lax.broadcasted_iota(jnp.int32, (tm + 2 * hal, 1), 0)
    inside = jnp.logical_and(jnp.logical_or(ridx >= hal, i > 0),
                             jnp.logical_or(ridx < hal + tm, i < pl.num_programs(1) - 1))
    hb = hb_ref[...]

    def branch(w_ref, b_ref, dw_ref, cb_ref):
        up = jnp.dot(hb, w_ref[...], preferred_element_type=F32) + b_ref[...]
        up = jnp.where(inside, up, 0.0)
        return (dw_ref[0:1, :] * up[hal - 1:hal - 1 + tm] + dw_ref[1:2, :] * up[hal:hal + tm]
                + dw_ref[2:3, :] * up[hal + 1:hal + 1 + tm] + cb_ref[...])

    g = branch(wg_ref, bg_ref, dwg_ref, cbg_ref)
    u = branch(wu_ref, bu_ref, dwu_ref, cbu_ref)
    y = (0.5 * g * (1.0 + lax.erf(g * (2.0 ** -0.5)))) * u
    acc_ref[...] += jnp.dot(y.astype(BF16), wd_ref[...], preferred_element_type=F32)

    @pl.when(j == pl.num_programs(2) - 1)
    def _():
        o_ref[...] = _layer_norm(alpha * h_ref[...] + acc_ref[...] + bd_ref[...],
                                 lw_ref[...], lb_ref[...])


def _ffn(h, w_up, b_up, w_dw, b_dw, w_down, b_down, lw, lb, batch, tm, tf, alpha):
    m, d = h.shape
    dff = w_down.shape[0]
    t = m // batch
    nt = t // tm
    nj = dff // tf
    hal = FFN_HALO
    row = lambda b, i, j: (b * nt + i, 0)
    prev = lambda b, i, j: (jnp.maximum((b * t + i * tm) // hal - 1, 0), 0)
    nxt = lambda b, i, j: (jnp.minimum((b * t + (i + 1) * tm) // hal, m // hal - 1), 0)
    gcol = lambda b, i, j: (0, j)
    ucol = lambda b, i, j: (0, nj + j)
    const = lambda b, i, j: (0, 0)
    return pl.pallas_call(
        functools.partial(_ffn_kernel, alpha=alpha),
        grid=(batch, nt, nj),
        in_specs=[pl.BlockSpec((tm, d), row), pl.BlockSpec((hal, d), prev), pl.BlockSpec((hal, d), nxt),
                  pl.BlockSpec((d, tf), gcol), pl.BlockSpec((d, tf), ucol),
                  pl.BlockSpec((1, tf), gcol), pl.BlockSpec((1, tf), ucol),
                  pl.BlockSpec((3, tf), gcol), pl.BlockSpec((3, tf), ucol),
                  pl.BlockSpec((1, tf), gcol), pl.BlockSpec((1, tf), ucol),
                  pl.BlockSpec((tf, d), lambda b, i, j: (j, 0))]
                 + [pl.BlockSpec((1, d), const)] * 3,
        out_specs=pl.BlockSpec((tm, d), row),
        out_shape=jax.ShapeDtypeStruct((m, d), F32),
        scratch_shapes=[pltpu.VMEM((tm + 2 * hal, d), BF16), pltpu.VMEM((tm, d), F32)],
        compiler_params=_params("parallel", "parallel", "arbitrary"),
        name="conv_ffn_ln",
    )(h, h, h, w_up, w_up, b_up, b_up, w_dw, w_dw, b_dw, b_dw, w_down, b_down, lw, lb)


def _pick(n, prefs):
    for p in prefs:
        if n % p == 0:
            return p
    return n


def kernel(x, mem, ln_in_w, ln_in_b, w_in, b_in, ml_norm_w, na_rpb, w_mix_out, b_mix_out,
           ln1_w, ln1_b, w_xq, w_xkv, w_xo, b_xo, ln2_w, ln2_b,
           w_up, b_up, w_dw, b_dw, w_down, b_down, ln3_w, ln3_b):
    batch, t, d = x.shape
    depth = w_in.shape[0]
    m = batch * t
    ml_width = d // 2
    na_width = d - ml_width
    dff = w_down.shape[1]
    alpha = (2.0 * depth) ** 0.25
    assert ml_width == na_width and t % (GRID_W * NA_ROWS) == 0 and t % ML_CHUNK == 0

    tm = _pick(t, (512, 256, 128))
    tm_ffn = _pick(t, (1024, 512, 256, 128))
    tf = _pick(dff, (256, 128))

    row2 = lambda a: a.reshape(1, -1)
    g0 = 4 * ml_width
    g1 = g0 + 4 * ML_HEADS

    h = _input_ln(x.reshape(m, d), row2(ln_in_w), row2(ln_in_b), tm)
    kv_all = _kvproj(mem.reshape(-1, d), w_xkv.astype(BF16))

    for l in range(depth):
        k0, k1 = ml_width, 2 * ml_width
        w_main = jnp.concatenate([w_in[l, :, :k0], w_in[l, :, k1:g0], w_in[l, :, g1:]],
                                 axis=1).astype(BF16)
        b_main = row2(jnp.concatenate([b_in[l, :k0], b_in[l, k1:g0], b_in[l, g1:]]))
        wk_t = w_in[l, :, k0:k1].T.astype(BF16)
        bk = b_in[l, k0:k1].reshape(-1, 1)
        wg_t = w_in[l, :, g0:g1].T.astype(BF16)
        bg = b_in[l, g0:g1].reshape(-1, 1)
        mq, mv, mo, nq, nk, nv, mk_t, gates = _inproj(h, w_main, b_main, wk_t, bk, wg_t, bg,
                                                     tm, ml_width)
        hf, hb = _mlstm(mq, mk_t, mv, gates, batch, ML_HEADS)
        na = _na(nq, nk, nv, _na_bias_table(na_rpb[l]), batch, NA_HEADS)
        h = _outproj(h, hf, hb, mo, na, row2(ml_norm_w[l]), w_mix_out[l].astype(BF16),
                     row2(b_mix_out[l]), row2(ln1_w[l]), row2(ln1_b[l]), tm, ML_HEADS, alpha)
        h = _xattn(h, kv_all[l], w_xq[l].astype(BF16), w_xo[l].astype(BF16), row2(b_xo[l]),
                   row2(ln2_w[l]), row2(ln2_b[l]), batch, tm, XA_HEADS, alpha)
        h = _ffn(h, w_up[l].astype(BF16), row2(b_up[l]), w_dw[l], row2(b_dw[l]),
                 w_down[l].astype(BF16), row2(b_down[l]), row2(ln3_w[l]), row2(ln3_b[l]),
                 batch, tm_ffn, tf, alpha)
    return h.reshape(batch, t, d)
```

```python
import functools

import jax
import jax.numpy as jnp
from jax import lax
from jax.experimental import pallas as pl
from jax.experimental.pallas import tpu as pltpu

F32 = jnp.float32
BF16 = jnp.bfloat16

LN_EPS = 1e-5
GRID_W = 64
ML_HEADS = 4
ML_CHUNK = 128
NA_HEADS = 8
NA_ROWS = 8
NA_COLS = 16
XA_HEADS = 4

VMEM_LIMIT_BYTES = 56 * 1024 * 1024

_NT = (((1,), (1,)), ((), ()))
_TN = (((0,), (0,)), ((), ()))


def _params(*sem):
    return pltpu.CompilerParams(dimension_semantics=sem, vmem_limit_bytes=VMEM_LIMIT_BYTES)


def _layer_norm(x, w, b):
    mu = jnp.mean(x, axis=-1, keepdims=True)
    xc = x - mu
    var = jnp.mean(xc * xc, axis=-1, keepdims=True)
    return xc * lax.rsqrt(var + LN_EPS) * w + b


def _ln_kernel(x_ref, w_ref, b_ref, o_ref):
    o_ref[...] = _layer_norm(x_ref[...], w_ref[...], b_ref[...])


def _input_ln(x, w, b, tm):
    m, d = x.shape
    return pl.pallas_call(
        _ln_kernel,
        grid=(m // tm,),
        in_specs=[pl.BlockSpec((tm, d), lambda i: (i, 0)),
                  pl.BlockSpec((1, d), lambda i: (0, 0)),
                  pl.BlockSpec((1, d), lambda i: (0, 0))],
        out_specs=pl.BlockSpec((tm, d), lambda i: (i, 0)),
        out_shape=jax.ShapeDtypeStruct((m, d), F32),
        compiler_params=_params("parallel"),
        name="input_ln",
    )(x, w, b)


def _scan_chunks(x, reverse, op, fill):
    n = x.shape[-1]
    pos = lax.broadcasted_iota(jnp.int32, x.shape, 1) % ML_CHUNK
    k = 1
    while k < ML_CHUNK:
        if reverse:
            x = op(x, jnp.where(pos < ML_CHUNK - k, pltpu.roll(x, n - k, axis=1), fill))
        else:
            x = op(x, jnp.where(pos >= k, pltpu.roll(x, k, axis=1), fill))
        k *= 2
    return x


def _inproj_kernel(h_ref, w_ref, b_ref, wk_ref, bk_ref, wg_ref, bg_ref,
                   mq_ref, mv_ref, mo_ref, nq_ref, nk_ref, nv_ref, kt_ref, g_ref, *, width):
    hb = h_ref[...].astype(BF16)
    g = lax.dot_general(wg_ref[...], hb, _NT, preferred_element_type=F32) + bg_ref[...]
    heads = g.shape[0] // 4
    rows = []
    for d in range(2):
        ig = g[2 * d * heads:(2 * d + 1) * heads, :]
        lf = jax.nn.log_sigmoid(g[(2 * d + 1) * heads:(2 * d + 2) * heads, :])
        a = _scan_chunks(lf, d == 1, jnp.add, 0.0)
        b = ig - a
        rows += [a, b, _scan_chunks(b, d == 1, jnp.maximum, -jnp.inf)]
    g_ref[...] = jnp.concatenate(rows, axis=0)
    outs = (mq_ref, mv_ref, mo_ref, nq_ref, nk_ref, nv_ref)
    for j, o_ref in enumerate(outs):
        cols = slice(j * width, (j + 1) * width)
        acc = jnp.dot(hb, w_ref[:, cols], preferred_element_type=F32) + b_ref[:, cols]
        o_ref[...] = acc.astype(o_ref.dtype)
    kt = lax.dot_general(wk_ref[...], hb, _NT, preferred_element_type=F32) + bk_ref[...]
    kt_ref[...] = kt.astype(kt_ref.dtype)


def _inproj(h, w_main, b_main, wk_t, bk, wg_t, bg, tm, width):
    m, d = h.shape
    ncol = w_main.shape[1]
    ng = wg_t.shape[0]
    ns = 6 * (ng // 4)
    row = lambda i: (i, 0)
    col = lambda i: (0, i)
    const = lambda i: (0, 0)
    seg = lambda dt: jax.ShapeDtypeStruct((m, width), dt)
    seg_spec = pl.BlockSpec((tm, width), row)
    return pl.pallas_call(
        functools.partial(_inproj_kernel, width=width),
        grid=(m // tm,),
        in_specs=[pl.BlockSpec((tm, d), row),
                  pl.BlockSpec((d, ncol), const),
                  pl.BlockSpec((1, ncol), const),
                  pl.BlockSpec((width, d), const),
                  pl.BlockSpec((width, 1), const),
                  pl.BlockSpec((ng, d), const),
                  pl.BlockSpec((ng, 1), const)],
        out_specs=[seg_spec] * 6 + [pl.BlockSpec((width, tm), col), pl.BlockSpec((ns, tm), col)],
        out_shape=[seg(BF16), seg(BF16), seg(F32), seg(BF16), seg(BF16), seg(BF16),
                   jax.ShapeDtypeStruct((width, m), BF16), jax.ShapeDtypeStruct((ns, m), F32)],
        compiler_params=_params("parallel"),
        name="mixer_inproj",
    )(h, w_main, b_main, wk_t, bk, wg_t, bg)


def _mlstm_kernel(qf_ref, ktf_ref, vf_ref, gf_ref, qb_ref, ktb_ref, vb_ref, gb_ref,
                  hf_ref, hb_ref, c_ref, m_ref, *, heads, dh):
    L = ML_CHUNK
    scale = dh ** -0.5

    @pl.when(pl.program_id(1) == 0)
    def _():
        c_ref[...] = jnp.zeros_like(c_ref)
        m_ref[...] = jnp.zeros_like(m_ref)

    t_idx = lax.broadcasted_iota(jnp.int32, (L, L), 0)
    s_idx = lax.broadcasted_iota(jnp.int32, (L, L), 1)

    def col_replicated(x_row):
        return jnp.broadcast_to(x_row, (L, L)).T

    ones = jnp.ones((L, dh), BF16)
    probs = []
    for d, (q_ref, kt_ref, v_ref, g_ref, o_ref) in enumerate(
            ((qf_ref, ktf_ref, vf_ref, gf_ref, hf_ref), (qb_ref, ktb_ref, vb_ref, gb_ref, hb_ref))):
        reverse = d == 1
        r0 = 3 * d * heads
        a = g_ref[r0:r0 + heads, :]
        b = g_ref[r0 + heads:r0 + 2 * heads, :]
        b_run = g_ref[r0 + 2 * heads:r0 + 3 * heads, :]
        m_prev = m_ref[d]
        g_run = jnp.maximum(m_prev, b_run)
        a_end = a[:, 0:1] if reverse else a[:, L - 1:L]
        up_log = a_end + b
        m_new = jnp.maximum(a_end + m_prev, jnp.max(up_log, axis=1, keepdims=True))
        wk = jnp.exp(up_log - m_new) * scale
        decay = jnp.exp(a_end + m_prev - m_new)
        m_ref[d] = m_new
        causal = (s_idx >= t_idx) if reverse else (s_idx <= t_idx)
        for hd in range(heads):
            row = slice(hd, hd + 1)
            probs.append(dict(
                st=d * heads + hd, cols=slice(hd * dh, (hd + 1) * dh), causal=causal,
                q_ref=q_ref, kt_ref=kt_ref, v_ref=v_ref, o_ref=o_ref,
                a=a[row], b=b[row], g=g_run[row], m_prev=m_prev[row], wk=wk[row], decay=decay[row]))

    for p in probs:
        p["a_cb"] = col_replicated(p["a"])
        p["g_cb"] = col_replicated(p["g"])
    for p in probs:
        p["q"] = p["q_ref"][:, p["cols"]]
        p["s"] = jnp.dot(p["q"], p["kt_ref"][p["cols"], :], preferred_element_type=F32)
    for p in probs:
        w = jnp.where(p["causal"], jnp.exp(p["b"] - p["g_cb"]), 0.0)
        p["w_qk"] = (w * (p["s"] * scale)).astype(BF16)
        p["inter"] = jnp.exp(p["m_prev"] - p["g_cb"])
        p["floor"] = jnp.exp(-(p["a_cb"] + p["g_cb"]))
    for p in probs:
        p["v_aug"] = jnp.concatenate([p["v_ref"][:, p["cols"]], ones], axis=1)
        p["c_aug"] = c_ref[p["st"]]
        q_c = jnp.dot(p["q"], p["c_aug"].astype(BF16), preferred_element_type=F32)
        w_v = jnp.dot(p["w_qk"], p["v_aug"], preferred_element_type=F32)
        num = w_v[:, :dh] + p["inter"] * q_c[:, :dh]
        den = w_v[:, dh:] + p["inter"] * q_c[:, dh:]
        p["o_ref"][:, p["cols"]] = num / jnp.maximum(jnp.abs(den), p["floor"])
    for p in probs:
        kw_t = (p["kt_ref"][p["cols"], :].astype(F32) * p["wk"]).astype(BF16)
        decay2 = jnp.concatenate([p["decay"], p["decay"]], axis=1)
        c_ref[p["st"]] = decay2 * p["c_aug"] + jnp.dot(kw_t, p["v_aug"], preferred_element_type=F32)


def _mlstm(mq, mk_t, mv, gates, batch, heads):
    m, width = mq.shape
    dh = width // heads
    nc = m // batch // ML_CHUNK
    ng = gates.shape[0]
    fwd = lambda b, c: (b * nc + c, 0)
    bwd = lambda b, c: (b * nc + nc - 1 - c, 0)
    fwd_t = lambda b, c: (0, b * nc + c)
    bwd_t = lambda b, c: (0, b * nc + nc - 1 - c)
    blk = lambda im: pl.BlockSpec((ML_CHUNK, width), im)
    tblk = lambda im: pl.BlockSpec((width, ML_CHUNK), im)
    gblk = lambda im: pl.BlockSpec((ng, ML_CHUNK), im)
    return pl.pallas_call(
        functools.partial(_mlstm_kernel, heads=heads, dh=dh),
        grid=(batch, nc),
        in_specs=[blk(fwd), tblk(fwd_t), blk(fwd), gblk(fwd_t),
                  blk(bwd), tblk(bwd_t), blk(bwd), gblk(bwd_t)],
        out_specs=[blk(fwd), blk(bwd)],
        out_shape=[jax.ShapeDtypeStruct((m, width), F32)] * 2,
        scratch_shapes=[pltpu.VMEM((2 * heads, dh, 2 * dh), F32),
                        pltpu.VMEM((2, heads, ML_CHUNK), F32)],
        compiler_params=_params("parallel", "arbitrary"),
        name="mlstm_scan",
    )(mq, mk_t, mv, gates, mq, mk_t, mv, gates)


def _na_bias_table(rpb):
    j = jnp.arange(GRID_W)
    dc = jnp.clip(j[None, :] - j[:, None] + NA_COLS - 1, 0, 2 * NA_COLS - 2)
    cs = jnp.clip(j - NA_COLS // 2, 0, GRID_W - NA_COLS)
    ok = (j[None, :] >= cs[:, None]) & (j[None, :] < cs[:, None] + NA_COLS)
    onehot = (dc[..., None] == jnp.arange(2 * NA_COLS - 1)).astype(F32)
    cols = jnp.einsum('hdc,qkc->hdqk', rpb.astype(F32), onehot, precision=lax.Precision.HIGHEST)
    cols = jnp.where(ok[None, None], cols, -jnp.inf)
    tab = jnp.stack([cols[:, off:off + NA_ROWS] for off in range(NA_ROWS)])
    tab = tab.transpose(0, 1, 3, 2, 4)
    return tab.reshape(NA_ROWS, rpb.shape[0], GRID_W, NA_ROWS * GRID_W)


def _na_window_start(r, rows):
    return jnp.clip(r - NA_ROWS // 2, 0, rows - NA_ROWS)


NA_ROWS_PER_STEP = 4


def _na_kernel(q_ref, k_ref, v_ref, bias_ref, o_ref, *, rows, heads, dh):
    nkeys = NA_ROWS * GRID_W
    scale = dh ** -0.5
    pw = 2 * dh
    lane_lo = lax.broadcasted_iota(jnp.int32, (GRID_W, pw), 1) < dh
    ones = jnp.ones((nkeys, pw), BF16)
    probs = []
    for rr in range(NA_ROWS_PER_STEP):
        r = pl.program_id(1) * NA_ROWS_PER_STEP + rr
        start = _na_window_start(r, rows)
        tok0 = pl.multiple_of(start * GRID_W, GRID_W)
        off = NA_ROWS - 1 - (r - start)
        for hp in range(heads // 2):
            probs.append((rr, hp, tok0, off))
    s = []
    for rr, hp, tok0, off in probs:
        pcols = slice(hp * pw, (hp + 1) * pw)
        q = q_ref[rr * GRID_W:(rr + 1) * GRID_W, pcols]
        k = k_ref[pl.ds(tok0, nkeys), pcols]
        for half in range(2):
            qh = jnp.where(lane_lo if half == 0 else jnp.logical_not(lane_lo), q, jnp.zeros_like(q))
            s.append(lax.dot_general(qh, k, _NT, preferred_element_type=F32) * scale
                     + bias_ref[off, 2 * hp + half])
    mx = [jnp.max(x, axis=1, keepdims=True) for x in s]
    p = [jnp.exp(x - m).astype(BF16) for x, m in zip(s, mx)]
    outs = []
    for i, (rr, hp, tok0, off) in enumerate(probs):
        v_aug = jnp.concatenate([v_ref[pl.ds(tok0, nkeys), hp * pw:(hp + 1) * pw], ones], axis=1)
        r_lo = jnp.dot(p[2 * i], v_aug, preferred_element_type=F32)
        r_hi = jnp.dot(p[2 * i + 1], v_aug, preferred_element_type=F32)
        both = jnp.where(jnp.concatenate([lane_lo, lane_lo], axis=1), r_lo, r_hi)
        outs.append(both[:, :pw] / both[:, pw:])
    npair = heads // 2
    for rr in range(NA_ROWS_PER_STEP):
        o_ref[rr * GRID_W:(rr + 1) * GRID_W, :] = jnp.concatenate(
            outs[rr * npair:(rr + 1) * npair], axis=1).astype(o_ref.dtype)


def _na(nq, nk, nv, bias_tab, batch, heads):
    m, width = nq.shape
    t = m // batch
    rows = t // GRID_W
    dh = width // heads
    nkeys = NA_ROWS * GRID_W
    rps = NA_ROWS_PER_STEP
    steps = rows // rps
    qblk = pl.BlockSpec((rps * GRID_W, width), lambda b, r: (b * steps + r, 0))
    return pl.pallas_call(
        functools.partial(_na_kernel, rows=rows, heads=heads, dh=dh),
        grid=(batch, steps),
        in_specs=[qblk,
                  pl.BlockSpec((t, width), lambda b, r: (b, 0)),
                  pl.BlockSpec((t, width), lambda b, r: (b, 0)),
                  pl.BlockSpec((NA_ROWS, heads, GRID_W, nkeys), lambda b, r: (0, 0, 0, 0))],
        out_specs=qblk,
        out_shape=jax.ShapeDtypeStruct((m, width), BF16),
        compiler_params=_params("parallel", "arbitrary"),
        name="neighborhood_attn",
    )(nq, nk, nv, bias_tab)


def _outproj_kernel(h_ref, hf_ref, hb_ref, mo_ref, na_ref, nw_ref, w_ref, b_ref, lw_ref, lb_ref,
                    o_ref, *, heads, alpha):
    width = hf_ref.shape[1]
    dh = width // heads
    hm = hf_ref[...] + hb_ref[...]
    parts = []
    for hd in range(heads):
        x = hm[:, hd * dh:(hd + 1) * dh]
        mu = jnp.mean(x, axis=-1, keepdims=True)
        xc = x - mu
        var = jnp.mean(xc * xc, axis=-1, keepdims=True)
        parts.append(xc * lax.rsqrt(var + LN_EPS))
    hn = jnp.concatenate(parts, axis=1)
    ml = jax.nn.sigmoid(mo_ref[...]) * (hn * nw_ref[...])
    y = (jnp.dot(ml.astype(BF16), w_ref[0:width, :], preferred_element_type=F32)
         + jnp.dot(na_ref[...], w_ref[width:, :], preferred_element_type=F32) + b_ref[...])
    o_ref[...] = _layer_norm(alpha * h_ref[...] + y, lw_ref[...], lb_ref[...])


def _outproj(h, hf, hb, mo, na, nw, w, b, lw, lb, tm, heads, alpha):
    m, d = h.shape
    width = hf.shape[1]
    row = lambda i: (i, 0)
    const = lambda i: (0, 0)
    return pl.pallas_call(
        functools.partial(_outproj_kernel, heads=heads, alpha=alpha),
        grid=(m // tm,),
        in_specs=[pl.BlockSpec((tm, d), row)] + [pl.BlockSpec((tm, width), row)] * 4
                 + [pl.BlockSpec((1, width), const), pl.BlockSpec((d, d), const)]
                 + [pl.BlockSpec((1, d), const)] * 3,
        out_specs=pl.BlockSpec((tm, d), row),
        out_shape=jax.ShapeDtypeStruct((m, d), F32),
        compiler_params=_params("parallel"),
        name="mixer_outproj_ln",
    )(h, hf, hb, mo, na, nw, w, b, lw, lb)


def _kvproj_kernel(mem_ref, w_ref, o_ref):
    o_ref[0] = jnp.dot(mem_ref[...].astype(BF16), w_ref[0],
                       preferred_element_type=F32).astype(o_ref.dtype)


def _kvproj(mem, w_kv):
    mm, d = mem.shape
    depth, _, n = w_kv.shape
    return pl.pallas_call(
        _kvproj_kernel,
        grid=(depth,),
        in_specs=[pl.BlockSpec((mm, d), lambda l: (0, 0)),
                  pl.BlockSpec((1, d, n), lambda l: (l, 0, 0))],
        out_specs=pl.BlockSpec((1, mm, n), lambda l: (l, 0, 0)),
        out_shape=jax.ShapeDtypeStruct((depth, mm, n), BF16),
        compiler_params=_params("parallel"),
        name="memory_kv_proj",
    )(mem, w_kv)


def _xattn_kernel(h_ref, kv_ref, wq_ref, wo_ref, bo_ref, lw_ref, lb_ref, o_ref, *, heads, alpha):
    d = h_ref.shape[1]
    dh = d // heads
    scale = dh ** -0.5
    h = h_ref[...]
    q = jnp.dot(h.astype(BF16), wq_ref[...], preferred_element_type=F32).astype(BF16)
    outs = []
    for hd in range(heads):
        qh = q[:, hd * dh:(hd + 1) * dh]
        kh = kv_ref[:, hd * dh:(hd + 1) * dh]
        vh = kv_ref[:, d + hd * dh:d + (hd + 1) * dh]
        s = lax.dot_general(qh, kh, _NT, preferred_element_type=F32) * scale
        p = jnp.exp(s - jnp.max(s, axis=1, keepdims=True))
        denom = jnp.sum(p, axis=1, keepdims=True)
        o = jnp.dot(p.astype(BF16), vh, preferred_element_type=F32) / denom
        outs.append(o.astype(BF16))
    o_all = jnp.concatenate(outs, axis=1)
    y = jnp.dot(o_all, wo_ref[...], preferred_element_type=F32) + bo_ref[...]
    o_ref[...] = _layer_norm(alpha * h + y, lw_ref[...], lb_ref[...])


def _xattn(h, kv, wq, wo, bo, lw, lb, batch, tm, heads, alpha):
    m, d = h.shape
    nt = m // batch // tm
    mem_len = kv.shape[0] // batch
    row = lambda b, i: (b * nt + i, 0)
    const = lambda b, i: (0, 0)
    return pl.pallas_call(
        functools.partial(_xattn_kernel, heads=heads, alpha=alpha),
        grid=(batch, nt),
        in_specs=[pl.BlockSpec((tm, d), row),
                  pl.BlockSpec((mem_len, 2 * d), lambda b, i: (b, 0)),
                  pl.BlockSpec((d, d), const), pl.BlockSpec((d, d), const)]
                 + [pl.BlockSpec((1, d), const)] * 3,
        out_specs=pl.BlockSpec((tm, d), row),
        out_shape=jax.ShapeDtypeStruct((m, d), F32),
        compiler_params=_params("parallel", "parallel"),
        name="memory_xattn_ln",
    )(h, kv, wq, wo, bo, lw, lb)


FFN_HALO = 16
FFN_ROW_CHUNK = 256


def _ffn_kernel(h_ref, hp_ref, hn_ref, wg_ref, wu_ref, bg_ref, bu_ref, dwg_ref, dwu_ref,
                cbg_ref, cbu_ref, wd_ref, bd_ref, lw_ref, lb_ref, o_ref, hb_ref, x_ref, acc_ref, *, alpha):
    i = pl.program_id(1)
    j = pl.program_id(2)
    tm = h_ref.shape[0]
    hal = FFN_HALO

    @pl.when(j == 0)
    def _():
        hb_ref[0:hal, :] = hp_ref[...].astype(BF16)
        hb_ref[hal:hal + tm, :] = h_ref[...].astype(BF16)
        hb_ref[hal + tm:, :] = hn_ref[...].astype(BF16)
        acc_ref[...] = jnp.zeros_like(acc_ref)

    tf = wg_ref.shape[1]
    rc = FFN_ROW_CHUNK
    nchunk = tm // rc
    first = i == 0
    last = i == pl.num_programs(1) - 1

    sides = []
    for w_ref, b_ref, dw_ref, cb_ref, c0 in ((wg_ref, bg_ref, dwg_ref, cbg_ref, 0),
                                             (wu_ref, bu_ref, dwu_ref, cbu_ref, tf)):
        taps = [dw_ref[k:k + 1, :] for k in range(3)]
        bias = b_ref[...] * (taps[0] + taps[1] + taps[2]) + cb_ref[...]
        sides.append((w_ref, b_ref, taps, bias, slice(c0, c0 + tf)))

    def project(r):
        lo = 0 if r == 0 else 2 * hal + r * rc
        hi = 2 * hal + (r + 1) * rc
        hb = hb_ref[lo:hi, :]
        for w_ref, b_ref, _, _, cols in sides:
            x_ref[lo:hi, cols] = jnp.dot(hb, w_ref[...], preferred_element_type=F32)
            if r == 0:
                x_ref[0:hal, cols] = jnp.where(first, -b_ref[...], x_ref[0:hal, cols])
            if r == nchunk - 1:
                x_ref[hal + tm:, cols] = jnp.where(last, -b_ref[...], x_ref[hal + tm:, cols])

    def gate_and_down(r):
        s0 = hal + r * rc
        conv = []
        for _, _, taps, bias, cols in sides:
            conv.append(taps[0] * x_ref[s0 - 1:s0 - 1 + rc, cols] + taps[1] * x_ref[s0:s0 + rc, cols]
                        + taps[2] * x_ref[s0 + 1:s0 + 1 + rc, cols] + bias)
        g, u = conv
        y = (0.5 * g * (1.0 + lax.erf(g * (2.0 ** -0.5)))) * u
        acc_ref[r * rc:(r + 1) * rc, :] += jnp.dot(y.astype(BF16), wd_ref[...],
                                                   preferred_element_type=F32)

    project(0)
    for r in range(nchunk):
        if r + 1 < nchunk:
            project(r + 1)
        gate_and_down(r)

    @pl.when(j == pl.num_programs(2) - 1)
    def _():
        o_ref[...] = _layer_norm(alpha * h_ref[...] + acc_ref[...] + bd_ref[...],
                                 lw_ref[...], lb_ref[...])


def _ffn(h, w_up, b_up, w_dw, b_dw, w_down, b_down, lw, lb, batch, tm, tf, alpha):
    m, d = h.shape
    dff = w_down.shape[0]
    t = m // batch
    nt = t // tm
    nj = dff // tf
    hal = FFN_HALO
    row = lambda b, i, j: (b * nt + i, 0)
    prev = lambda b, i, j: (jnp.maximum((b * t + i * tm) // hal - 1, 0), 0)
    nxt = lambda b, i, j: (jnp.minimum((b * t + (i + 1) * tm) // hal, m // hal - 1), 0)
    gcol = lambda b, i, j: (0, j)
    ucol = lambda b, i, j: (0, nj + j)
    const = lambda b, i, j: (0, 0)
    return pl.pallas_call(
        functools.partial(_ffn_kernel, alpha=alpha),
        grid=(batch, nt, nj),
        in_specs=[pl.BlockSpec((tm, d), row), pl.BlockSpec((hal, d), prev), pl.BlockSpec((hal, d), nxt),
                  pl.BlockSpec((d, tf), gcol), pl.BlockSpec((d, tf), ucol),
                  pl.BlockSpec((1, tf), gcol), pl.BlockSpec((1, tf), ucol),
                  pl.BlockSpec((3, tf), gcol), pl.BlockSpec((3, tf), ucol),
                  pl.BlockSpec((1, tf), gcol), pl.BlockSpec((1, tf), ucol),
                  pl.BlockSpec((tf, d), lambda b, i, j: (j, 0))]
                 + [pl.BlockSpec((1, d), const)] * 3,
        out_specs=pl.BlockSpec((tm, d), row),
        out_shape=jax.ShapeDtypeStruct((m, d), F32),
        scratch_shapes=[pltpu.VMEM((tm + 2 * hal, d), BF16), pltpu.VMEM((tm + 2 * hal, 2 * tf), F32),
                        pltpu.VMEM((tm, d), F32)],
        compiler_params=_params("parallel", "parallel", "arbitrary"),
        name="conv_ffn_ln",
    )(h, h, h, w_up, w_up, b_up, b_up, w_dw, w_dw, b_dw, b_dw, w_down, b_down, lw, lb)


def _pick(n, prefs):
    for p in prefs:
        if n % p == 0:
            return p
    return n


def kernel(x, mem, ln_in_w, ln_in_b, w_in, b_in, ml_norm_w, na_rpb, w_mix_out, b_mix_out,
           ln1_w, ln1_b, w_xq, w_xkv, w_xo, b_xo, ln2_w, ln2_b,
           w_up, b_up, w_dw, b_dw, w_down, b_down, ln3_w, ln3_b):
    batch, t, d = x.shape
    depth = w_in.shape[0]
    m = batch * t
    ml_width = d // 2
    na_width = d - ml_width
    dff = w_down.shape[1]
    alpha = (2.0 * depth) ** 0.25
    assert ml_width == na_width and t % (GRID_W * NA_ROWS) == 0 and t % ML_CHUNK == 0

    tm = _pick(t, (512, 256, 128))
    tm_ffn = _pick(t, (1024, 512, 256, 128))
    tf = _pick(dff, (256, 128))

    row2 = lambda a: a.reshape(1, -1)
    g0 = 4 * ml_width
    g1 = g0 + 4 * ML_HEADS

    h = _input_ln(x.reshape(m, d), row2(ln_in_w), row2(ln_in_b), tm)
    kv_all = _kvproj(mem.reshape(-1, d), w_xkv.astype(BF16))

    for l in range(depth):
        k0, k1 = ml_width, 2 * ml_width
        w_main = jnp.concatenate([w_in[l, :, :k0], w_in[l, :, k1:g0], w_in[l, :, g1:]],
                                 axis=1).astype(BF16)
        b_main = row2(jnp.concatenate([b_in[l, :k0], b_in[l, k1:g0], b_in[l, g1:]]))
        wk_t = w_in[l, :, k0:k1].T.astype(BF16)
        bk = b_in[l, k0:k1].reshape(-1, 1)
        wg_t = w_in[l, :, g0:g1].T.astype(BF16)
        bg = b_in[l, g0:g1].reshape(-1, 1)
        mq, mv, mo, nq, nk, nv, mk_t, gates = _inproj(h, w_main, b_main, wk_t, bk, wg_t, bg,
                                                     tm, ml_width)
        hf, hb = _mlstm(mq, mk_t, mv, gates, batch, ML_HEADS)
        na = _na(nq, nk, nv, _na_bias_table(na_rpb[l]), batch, NA_HEADS)
        h = _outproj(h, hf, hb, mo, na, row2(ml_norm_w[l]), w_mix_out[l].astype(BF16),
                     row2(b_mix_out[l]), row2(ln1_w[l]), row2(ln1_b[l]), tm, ML_HEADS, alpha)
        h = _xattn(h, kv_all[l], w_xq[l].astype(BF16), w_xo[l].astype(BF16), row2(b_xo[l]),
                   row2(ln2_w[l]), row2(ln2_b[l]), batch, tm, XA_HEADS, alpha)
        h = _ffn(h, w_up[l].astype(BF16), row2(b_up[l]), w_dw[l], row2(b_dw[l]),
                 w_down[l].astype(BF16), row2(b_down[l]), row2(ln3_w[l]), row2(ln3_b[l]),
                 batch, tm_ffn, tf, alpha)
    return h.reshape(batch, t, d)
```

```python
import functools

import jax
import jax.numpy as jnp
from jax import lax
from jax.experimental import pallas as pl
from jax.experimental.pallas import tpu as pltpu

F32 = jnp.float32
BF16 = jnp.bfloat16

LN_EPS = 1e-5
GRID_W = 64
ML_HEADS = 4
ML_CHUNK = 128
NA_HEADS = 8
NA_ROWS = 8
NA_COLS = 16
XA_HEADS = 4

VMEM_LIMIT_BYTES = 56 * 1024 * 1024

_NT = (((1,), (1,)), ((), ()))
_TN = (((0,), (0,)), ((), ()))


def _params(*sem):
    return pltpu.CompilerParams(dimension_semantics=sem, vmem_limit_bytes=VMEM_LIMIT_BYTES)


def _layer_norm(x, w, b):
    mu = jnp.mean(x, axis=-1, keepdims=True)
    xc = x - mu
    var = jnp.mean(xc * xc, axis=-1, keepdims=True)
    return xc * lax.rsqrt(var + LN_EPS) * w + b


def _ln_kernel(x_ref, w_ref, b_ref, o_ref):
    o_ref[...] = _layer_norm(x_ref[...], w_ref[...], b_ref[...])


def _input_ln(x, w, b, tm):
    m, d = x.shape
    return pl.pallas_call(
        _ln_kernel,
        grid=(m // tm,),
        in_specs=[pl.BlockSpec((tm, d), lambda i: (i, 0)),
                  pl.BlockSpec((1, d), lambda i: (0, 0)),
                  pl.BlockSpec((1, d), lambda i: (0, 0))],
        out_specs=pl.BlockSpec((tm, d), lambda i: (i, 0)),
        out_shape=jax.ShapeDtypeStruct((m, d), F32),
        compiler_params=_params("parallel"),
        name="input_ln",
    )(x, w, b)


def _scan_chunks(x, reverse, op, fill):
    n = x.shape[-1]
    pos = lax.broadcasted_iota(jnp.int32, x.shape, 1) % ML_CHUNK
    k = 1
    while k < ML_CHUNK:
        if reverse:
            x = op(x, jnp.where(pos < ML_CHUNK - k, pltpu.roll(x, n - k, axis=1), fill))
        else:
            x = op(x, jnp.where(pos >= k, pltpu.roll(x, k, axis=1), fill))
        k *= 2
    return x


def _inproj_kernel(h_ref, w_ref, b_ref, wk_ref, bk_ref, wg_ref, bg_ref,
                   mq_ref, mv_ref, mo_ref, nq_ref, nk_ref, nv_ref, kt_ref, g_ref, *, width):
    hb = h_ref[...].astype(BF16)
    g = lax.dot_general(wg_ref[...], hb, _NT, preferred_element_type=F32) + bg_ref[...]
    heads = g.shape[0] // 4
    rows = []
    for d in range(2):
        ig = g[2 * d * heads:(2 * d + 1) * heads, :]
        lf = jax.nn.log_sigmoid(g[(2 * d + 1) * heads:(2 * d + 2) * heads, :])
        a = _scan_chunks(lf, d == 1, jnp.add, 0.0)
        b = ig - a
        rows += [a, b, _scan_chunks(b, d == 1, jnp.maximum, -jnp.inf)]
    g_ref[...] = jnp.concatenate(rows, axis=0)
    outs = (mq_ref, mv_ref, mo_ref, nq_ref, nk_ref, nv_ref)
    for j, o_ref in enumerate(outs):
        cols = slice(j * width, (j + 1) * width)
        acc = jnp.dot(hb, w_ref[:, cols], preferred_element_type=F32) + b_ref[:, cols]
        o_ref[...] = acc.astype(o_ref.dtype)
    kt = lax.dot_general(wk_ref[...], hb, _NT, preferred_element_type=F32) + bk_ref[...]
    kt_ref[...] = kt.astype(kt_ref.dtype)


def _inproj(h, l, w_main, b_main, wk_t, bk, wg_t, bg, tm, width):
    m, d = h.shape
    ncol = w_main.shape[2]
    ng = wg_t.shape[1]
    ns = 6 * (ng // 4)
    row = lambda i: (i, 0)
    col = lambda i: (0, i)
    layer = lambda i: (l, 0, 0)
    seg = lambda dt: jax.ShapeDtypeStruct((m, width), dt)
    seg_spec = pl.BlockSpec((tm, width), row)
    return pl.pallas_call(
        functools.partial(_inproj_kernel, width=width),
        grid=(m // tm,),
        in_specs=[pl.BlockSpec((tm, d), row),
                  pl.BlockSpec((None, d, ncol), layer),
                  pl.BlockSpec((None, 1, ncol), layer),
                  pl.BlockSpec((None, width, d), layer),
                  pl.BlockSpec((None, width, 1), layer),
                  pl.BlockSpec((None, ng, d), layer),
                  pl.BlockSpec((None, ng, 1), layer)],
        out_specs=[seg_spec] * 6 + [pl.BlockSpec((width, tm), col), pl.BlockSpec((ns, tm), col)],
        out_shape=[seg(BF16), seg(BF16), seg(F32), seg(BF16), seg(BF16), seg(BF16),
                   jax.ShapeDtypeStruct((width, m), BF16), jax.ShapeDtypeStruct((ns, m), F32)],
        compiler_params=_params("parallel"),
        name="mixer_inproj",
    )(h, w_main, b_main, wk_t, bk, wg_t, bg)


def _mlstm_kernel(qf_ref, ktf_ref, vf_ref, gf_ref, qb_ref, ktb_ref, vb_ref, gb_ref,
                  hf_ref, hb_ref, c_ref, m_ref, *, heads, dh):
    L = ML_CHUNK
    scale = dh ** -0.5

    @pl.when(pl.program_id(1) == 0)
    def _():
        c_ref[...] = jnp.zeros_like(c_ref)
        m_ref[...] = jnp.zeros_like(m_ref)

    t_idx = lax.broadcasted_iota(jnp.int32, (L, L), 0)
    s_idx = lax.broadcasted_iota(jnp.int32, (L, L), 1)

    def col_replicated(x_row):
        return jnp.broadcast_to(x_row, (L, L)).T

    ones = jnp.ones((L, dh), BF16)
    probs = []
    for d, (q_ref, kt_ref, v_ref, g_ref, o_ref) in enumerate(
            ((qf_ref, ktf_ref, vf_ref, gf_ref, hf_ref), (qb_ref, ktb_ref, vb_ref, gb_ref, hb_ref))):
        reverse = d == 1
        r0 = 3 * d * heads
        a = g_ref[r0:r0 + heads, :]
        b = g_ref[r0 + heads:r0 + 2 * heads, :]
        b_run = g_ref[r0 + 2 * heads:r0 + 3 * heads, :]
        m_prev = m_ref[d]
        g_run = jnp.maximum(m_prev, b_run)
        a_end = a[:, 0:1] if reverse else a[:, L - 1:L]
        up_log = a_end + b
        m_new = jnp.maximum(a_end + m_prev, jnp.max(up_log, axis=1, keepdims=True))
        wk = jnp.exp(up_log - m_new) * scale
        decay = jnp.exp(a_end + m_prev - m_new)
        m_ref[d] = m_new
        causal = (s_idx >= t_idx) if reverse else (s_idx <= t_idx)
        for hd in range(heads):
            row = slice(hd, hd + 1)
            probs.append(dict(
                st=d * heads + hd, cols=slice(hd * dh, (hd + 1) * dh), causal=causal,
                q_ref=q_ref, kt_ref=kt_ref, v_ref=v_ref, o_ref=o_ref,
                a=a[row], b=b[row], g=g_run[row], m_prev=m_prev[row], wk=wk[row], decay=decay[row]))

    for p in probs:
        p["a_cb"] = col_replicated(p["a"])
        p["g_cb"] = col_replicated(p["g"])
    for p in probs:
        p["q"] = p["q_ref"][:, p["cols"]]
        p["s"] = jnp.dot(p["q"], p["kt_ref"][p["cols"], :], preferred_element_type=F32)
    for p in probs:
        w = jnp.where(p["causal"], jnp.exp(p["b"] - p["g_cb"]), 0.0)
        p["w_qk"] = (w * (p["s"] * scale)).astype(BF16)
        p["inter"] = jnp.exp(p["m_prev"] - p["g_cb"])
        p["floor"] = jnp.exp(-(p["a_cb"] + p["g_cb"]))
    for p in probs:
        p["v_aug"] = jnp.concatenate([p["v_ref"][:, p["cols"]], ones], axis=1)
        p["c_aug"] = c_ref[p["st"]]
        q_c = jnp.dot(p["q"], p["c_aug"].astype(BF16), preferred_element_type=F32)
        w_v = jnp.dot(p["w_qk"], p["v_aug"], preferred_element_type=F32)
        num = w_v[:, :dh] + p["inter"] * q_c[:, :dh]
        den = w_v[:, dh:] + p["inter"] * q_c[:, dh:]
        p["o_ref"][:, p["cols"]] = num / jnp.maximum(jnp.abs(den), p["floor"])
    for p in probs:
        kw_t = (p["kt_ref"][p["cols"], :].astype(F32) * p["wk"]).astype(BF16)
        decay2 = jnp.concatenate([p["decay"], p["decay"]], axis=1)
        c_ref[p["st"]] = decay2 * p["c_aug"] + jnp.dot(kw_t, p["v_aug"], preferred_element_type=F32)


def _mlstm(mq, mk_t, mv, gates, batch, heads):
    m, width = mq.shape
    dh = width // heads
    nc = m // batch // ML_CHUNK
    ng = gates.shape[0]
    fwd = lambda b, c: (b * nc + c, 0)
    bwd = lambda b, c: (b * nc + nc - 1 - c, 0)
    fwd_t = lambda b, c: (0, b * nc + c)
    bwd_t = lambda b, c: (0, b * nc + nc - 1 - c)
    blk = lambda im: pl.BlockSpec((ML_CHUNK, width), im)
    tblk = lambda im: pl.BlockSpec((width, ML_CHUNK), im)
    gblk = lambda im: pl.BlockSpec((ng, ML_CHUNK), im)
    return pl.pallas_call(
        functools.partial(_mlstm_kernel, heads=heads, dh=dh),
        grid=(batch, nc),
        in_specs=[blk(fwd), tblk(fwd_t), blk(fwd), gblk(fwd_t),
                  blk(bwd), tblk(bwd_t), blk(bwd), gblk(bwd_t)],
        out_specs=[blk(fwd), blk(bwd)],
        out_shape=[jax.ShapeDtypeStruct((m, width), F32)] * 2,
        scratch_shapes=[pltpu.VMEM((2 * heads, dh, 2 * dh), F32),
                        pltpu.VMEM((2, heads, ML_CHUNK), F32)],
        compiler_params=_params("parallel", "arbitrary"),
        name="mlstm_scan",
    )(mq, mk_t, mv, gates, mq, mk_t, mv, gates)


def _na_bias_table(rpb):
    j = jnp.arange(GRID_W)
    dc = jnp.clip(j[None, :] - j[:, None] + NA_COLS - 1, 0, 2 * NA_COLS - 2)
    cs = jnp.clip(j - NA_COLS // 2, 0, GRID_W - NA_COLS)
    ok = (j[None, :] >= cs[:, None]) & (j[None, :] < cs[:, None] + NA_COLS)
    onehot = (dc[..., None] == jnp.arange(2 * NA_COLS - 1)).astype(F32)
    cols = jnp.einsum('hdc,qkc->hdqk', rpb.astype(F32), onehot, precision=lax.Precision.HIGHEST)
    cols = jnp.where(ok[None, None], cols, -jnp.inf)
    tab = jnp.stack([cols[:, off:off + NA_ROWS] for off in range(NA_ROWS)])
    tab = tab.transpose(0, 1, 3, 2, 4)
    return tab.reshape(NA_ROWS, rpb.shape[0], GRID_W, NA_ROWS * GRID_W)


def _na_window_start(r, rows):
    return jnp.clip(r - NA_ROWS // 2, 0, rows - NA_ROWS)


NA_ROWS_PER_STEP = 4


def _na_kernel(q_ref, k_ref, v_ref, bias_ref, o_ref, *, rows, heads, dh):
    nkeys = NA_ROWS * GRID_W
    scale = dh ** -0.5
    pw = 2 * dh
    lane_lo = lax.broadcasted_iota(jnp.int32, (GRID_W, pw), 1) < dh
    ones = jnp.ones((nkeys, pw), BF16)
    probs = []
    for rr in range(NA_ROWS_PER_STEP):
        r = pl.program_id(1) * NA_ROWS_PER_STEP + rr
        start = _na_window_start(r, rows)
        tok0 = pl.multiple_of(start * GRID_W, GRID_W)
        off = NA_ROWS - 1 - (r - start)
        for hp in range(heads // 2):
            probs.append((rr, hp, tok0, off))
    s = []
    for rr, hp, tok0, off in probs:
        pcols = slice(hp * pw, (hp + 1) * pw)
        q = q_ref[rr * GRID_W:(rr + 1) * GRID_W, pcols]
        k = k_ref[pl.ds(tok0, nkeys), pcols]
        for half in range(2):
            qh = jnp.where(lane_lo if half == 0 else jnp.logical_not(lane_lo), q, jnp.zeros_like(q))
            s.append(lax.dot_general(qh, k, _NT, preferred_element_type=F32) * scale
                     + bias_ref[off, 2 * hp + half])
    mx = [jnp.max(x, axis=1, keepdims=True) for x in s]
    p = [jnp.exp(x - m).astype(BF16) for x, m in zip(s, mx)]
    outs = []
    for i, (rr, hp, tok0, off) in enumerate(probs):
        v_aug = jnp.concatenate([v_ref[pl.ds(tok0, nkeys), hp * pw:(hp + 1) * pw], ones], axis=1)
        r_lo = jnp.dot(p[2 * i], v_aug, preferred_element_type=F32)
        r_hi = jnp.dot(p[2 * i + 1], v_aug, preferred_element_type=F32)
        both = jnp.where(jnp.concatenate([lane_lo, lane_lo], axis=1), r_lo, r_hi)
        outs.append(both[:, :pw] / both[:, pw:])
    npair = heads // 2
    for rr in range(NA_ROWS_PER_STEP):
        o_ref[rr * GRID_W:(rr + 1) * GRID_W, :] = jnp.concatenate(
            outs[rr * npair:(rr + 1) * npair], axis=1).astype(o_ref.dtype)


def _na(nq, nk, nv, l, bias_tab, batch, heads):
    m, width = nq.shape
    t = m // batch
    rows = t // GRID_W
    dh = width // heads
    nkeys = NA_ROWS * GRID_W
    rps = NA_ROWS_PER_STEP
    steps = rows // rps
    qblk = pl.BlockSpec((rps * GRID_W, width), lambda b, r: (b * steps + r, 0))
    return pl.pallas_call(
        functools.partial(_na_kernel, rows=rows, heads=heads, dh=dh),
        grid=(batch, steps),
        in_specs=[qblk,
                  pl.BlockSpec((t, width), lambda b, r: (b, 0)),
                  pl.BlockSpec((t, width), lambda b, r: (b, 0)),
                  pl.BlockSpec((None, NA_ROWS, heads, GRID_W, nkeys), lambda b, r: (l, 0, 0, 0, 0))],
        out_specs=qblk,
        out_shape=jax.ShapeDtypeStruct((m, width), BF16),
        compiler_params=_params("parallel", "arbitrary"),
        name="neighborhood_attn",
    )(nq, nk, nv, bias_tab)


def _outproj_kernel(h_ref, hf_ref, hb_ref, mo_ref, na_ref, nw_ref, w_ref, b_ref, lw_ref, lb_ref,
                    o_ref, *, heads, alpha):
    width = hf_ref.shape[1]
    dh = width // heads
    hm = hf_ref[...] + hb_ref[...]
    parts = []
    for hd in range(heads):
        x = hm[:, hd * dh:(hd + 1) * dh]
        mu = jnp.mean(x, axis=-1, keepdims=True)
        xc = x - mu
        var = jnp.mean(xc * xc, axis=-1, keepdims=True)
        parts.append(xc * lax.rsqrt(var + LN_EPS))
    hn = jnp.concatenate(parts, axis=1)
    ml = jax.nn.sigmoid(mo_ref[...]) * (hn * nw_ref[...])
    y = (jnp.dot(ml.astype(BF16), w_ref[0:width, :], preferred_element_type=F32)
         + jnp.dot(na_ref[...], w_ref[width:, :], preferred_element_type=F32) + b_ref[...])
    o_ref[...] = _layer_norm(alpha * h_ref[...] + y, lw_ref[...], lb_ref[...])


def _outproj(h, hf, hb, mo, na, l, nw, w, b, lw, lb, tm, heads, alpha):
    m, d = h.shape
    width = hf.shape[1]
    row = lambda i: (i, 0)
    layer = lambda i: (l, 0, 0)
    return pl.pallas_call(
        functools.partial(_outproj_kernel, heads=heads, alpha=alpha),
        grid=(m // tm,),
        in_specs=[pl.BlockSpec((tm, d), row)] + [pl.BlockSpec((tm, width), row)] * 4
                 + [pl.BlockSpec((None, 1, width), layer), pl.BlockSpec((None, d, d), layer)]
                 + [pl.BlockSpec((None, 1, d), layer)] * 3,
        out_specs=pl.BlockSpec((tm, d), row),
        out_shape=jax.ShapeDtypeStruct((m, d), F32),
        compiler_params=_params("parallel"),
        name="mixer_outproj_ln",
    )(h, hf, hb, mo, na, nw, w, b, lw, lb)


def _kvproj_kernel(mem_ref, w_ref, o_ref):
    o_ref[0] = jnp.dot(mem_ref[...].astype(BF16), w_ref[0],
                       preferred_element_type=F32).astype(o_ref.dtype)


def _kvproj(mem, w_kv):
    mm, d = mem.shape
    depth, _, n = w_kv.shape
    return pl.pallas_call(
        _kvproj_kernel,
        grid=(depth,),
        in_specs=[pl.BlockSpec((mm, d), lambda l: (0, 0)),
                  pl.BlockSpec((1, d, n), lambda l: (l, 0, 0))],
        out_specs=pl.BlockSpec((1, mm, n), lambda l: (l, 0, 0)),
        out_shape=jax.ShapeDtypeStruct((depth, mm, n), BF16),
        compiler_params=_params("parallel"),
        name="memory_kv_proj",
    )(mem, w_kv)


def _xattn_kernel(h_ref, kv_ref, wq_ref, wo_ref, bo_ref, lw_ref, lb_ref, o_ref, *, heads, alpha):
    d = h_ref.shape[1]
    dh = d // heads
    scale = dh ** -0.5
    h = h_ref[...]
    q = jnp.dot(h.astype(BF16), wq_ref[...], preferred_element_type=F32).astype(BF16)
    outs = []
    for hd in range(heads):
        qh = q[:, hd * dh:(hd + 1) * dh]
        kh = kv_ref[:, hd * dh:(hd + 1) * dh]
        vh = kv_ref[:, d + hd * dh:d + (hd + 1) * dh]
        s = lax.dot_general(qh, kh, _NT, preferred_element_type=F32) * scale
        p = jnp.exp(s - jnp.max(s, axis=1, keepdims=True))
        denom = jnp.sum(p, axis=1, keepdims=True)
        o = jnp.dot(p.astype(BF16), vh, preferred_element_type=F32) / denom
        outs.append(o.astype(BF16))
    o_all = jnp.concatenate(outs, axis=1)
    y = jnp.dot(o_all, wo_ref[...], preferred_element_type=F32) + bo_ref[...]
    o_ref[...] = _layer_norm(alpha * h + y, lw_ref[...], lb_ref[...])


def _xattn(h, l, kv, wq, wo, bo, lw, lb, batch, tm, heads, alpha):
    m, d = h.shape
    nt = m // batch // tm
    mem_len = kv.shape[1] // batch
    row = lambda b, i: (b * nt + i, 0)
    layer = lambda b, i: (l, 0, 0)
    return pl.pallas_call(
        functools.partial(_xattn_kernel, heads=heads, alpha=alpha),
        grid=(batch, nt),
        in_specs=[pl.BlockSpec((tm, d), row),
                  pl.BlockSpec((None, mem_len, 2 * d), lambda b, i: (l, b, 0)),
                  pl.BlockSpec((None, d, d), layer), pl.BlockSpec((None, d, d), layer)]
                 + [pl.BlockSpec((None, 1, d), layer)] * 3,
        out_specs=pl.BlockSpec((tm, d), row),
        out_shape=jax.ShapeDtypeStruct((m, d), F32),
        compiler_params=_params("parallel", "parallel"),
        name="memory_xattn_ln",
    )(h, kv, wq, wo, bo, lw, lb)


FFN_HALO = 16
FFN_SUB = 128


def _ffn_interleave(a, dff):
    lead = a.shape[:-1]
    a = a.reshape(lead + (2, dff // FFN_SUB, FFN_SUB))
    return jnp.swapaxes(a, -3, -2).reshape(lead + (2 * dff,))


def _ffn_kernel(h_ref, hp_ref, hn_ref, w_ref, b_ref, dw_ref, cb_ref, wd_ref, bd_ref, lw_ref, lb_ref,
                o_ref, hb_ref, x_ref, y_ref, *, alpha):
    i = pl.program_id(1)
    tm = h_ref.shape[0]
    hal = FFN_HALO
    sb = FFN_SUB
    nsub = y_ref.shape[1] // sb
    first = i == 0
    last = i == pl.num_programs(1) - 1

    hb_ref[0:hal, :] = hp_ref[...].astype(BF16)
    hb_ref[hal:hal + tm, :] = h_ref[...].astype(BF16)
    hb_ref[hal + tm:, :] = hn_ref[...].astype(BF16)

    def project(c):
        cols = slice(2 * sb * c, 2 * sb * (c + 1))
        x = jnp.dot(hb_ref[...], w_ref[:, cols], preferred_element_type=F32)
        nb = -b_ref[:, cols]
        x_ref[c % 2, 0:hal, :] = jnp.where(first, nb, x[0:hal])
        x_ref[c % 2, hal:hal + tm, :] = x[hal:hal + tm]
        x_ref[c % 2, hal + tm:, :] = jnp.where(last, nb, x[hal + tm:])

    def gate(c):
        cols = slice(2 * sb * c, 2 * sb * (c + 1))
        taps = [dw_ref[k:k + 1, cols] for k in range(3)]
        bias = b_ref[:, cols] * (taps[0] + taps[1] + taps[2]) + cb_ref[:, cols]
        xs = x_ref.at[c % 2]
        conv = (taps[0] * xs[hal - 1:hal - 1 + tm, :] + taps[1] * xs[hal:hal + tm, :]
                + taps[2] * xs[hal + 1:hal + 1 + tm, :] + bias)
        g, u = conv[:, :sb], conv[:, sb:]
        y = (0.5 * g * (1.0 + lax.erf(g * (2.0 ** -0.5)))) * u
        y_ref[:, c * sb:(c + 1) * sb] = y.astype(BF16)

    project(0)
    for c in range(nsub):
        if c + 1 < nsub:
            project(c + 1)
        gate(c)
    down = jnp.dot(y_ref[...], wd_ref[...], preferred_element_type=F32)
    o_ref[...] = _layer_norm(alpha * h_ref[...] + down + bd_ref[...], lw_ref[...], lb_ref[...])


def _ffn(h, l, w_gu, b_gu, dw_gu, cb_gu, w_down, b_down, lw, lb, batch, tm, alpha):
    m, d = h.shape
    dff = w_down.shape[1]
    t = m // batch
    nt = t // tm
    hal = FFN_HALO
    row = lambda b, i: (b * nt + i, 0)
    prev = lambda b, i: (jnp.maximum((b * t + i * tm) // hal - 1, 0), 0)
    nxt = lambda b, i: (jnp.minimum((b * t + (i + 1) * tm) // hal, m // hal - 1), 0)
    layer = lambda b, i: (l, 0, 0)
    resident = dict(pipeline_mode=pl.Buffered(1))
    return pl.pallas_call(
        functools.partial(_ffn_kernel, alpha=alpha),
        grid=(batch, nt),
        in_specs=[pl.BlockSpec((tm, d), row), pl.BlockSpec((hal, d), prev), pl.BlockSpec((hal, d), nxt),
                  pl.BlockSpec((None, d, 2 * dff), layer, **resident),
                  pl.BlockSpec((None, 1, 2 * dff), layer),
                  pl.BlockSpec((None, 3, 2 * dff), layer),
                  pl.BlockSpec((None, 1, 2 * dff), layer),
                  pl.BlockSpec((None, dff, d), layer, **resident)]
                 + [pl.BlockSpec((None, 1, d), layer)] * 3,
        out_specs=pl.BlockSpec((tm, d), row),
        out_shape=jax.ShapeDtypeStruct((m, d), F32),
        scratch_shapes=[pltpu.VMEM((tm + 2 * hal, d), BF16),
                        pltpu.VMEM((2, tm + 2 * hal, 2 * FFN_SUB), F32),
                        pltpu.VMEM((tm, dff), BF16)],
        compiler_params=_params("parallel", "arbitrary"),
        name="conv_ffn_ln",
    )(h, h, h, w_gu, b_gu, dw_gu, cb_gu, w_down, b_down, lw, lb)


def _pick(n, prefs):
    for p in prefs:
        if n % p == 0:
            return p
    return n


def kernel(x, mem, ln_in_w, ln_in_b, w_in, b_in, ml_norm_w, na_rpb, w_mix_out, b_mix_out,
           ln1_w, ln1_b, w_xq, w_xkv, w_xo, b_xo, ln2_w, ln2_b,
           w_up, b_up, w_dw, b_dw, w_down, b_down, ln3_w, ln3_b):
    batch, t, d = x.shape
    depth = w_in.shape[0]
    m = batch * t
    ml_width = d // 2
    na_width = d - ml_width
    dff = w_down.shape[1]
    alpha = (2.0 * depth) ** 0.25
    assert ml_width == na_width and t % (GRID_W * NA_ROWS) == 0 and t % ML_CHUNK == 0

    tm = _pick(t, (512, 256, 128))

    row2 = lambda a: a.reshape(1, -1)
    row3 = lambda a: a[:, None, :]
    col3 = lambda a: a[:, :, None]
    bf = lambda a: a.astype(BF16)
    k0, k1 = ml_width, 2 * ml_width
    g0 = 4 * ml_width
    g1 = g0 + 4 * ML_HEADS
    w_main = bf(jnp.concatenate([w_in[:, :, :k0], w_in[:, :, k1:g0], w_in[:, :, g1:]], axis=2))
    b_main = row3(jnp.concatenate([b_in[:, :k0], b_in[:, k1:g0], b_in[:, g1:]], axis=1))
    wk_t = bf(jnp.swapaxes(w_in[:, :, k0:k1], 1, 2))
    bk = col3(b_in[:, k0:k1])
    wg_t = bf(jnp.swapaxes(w_in[:, :, g0:g1], 1, 2))
    bg = col3(b_in[:, g0:g1])
    bias_tab = jax.vmap(_na_bias_table)(na_rpb)
    w_out, w_q, w_o = bf(w_mix_out), bf(w_xq), bf(w_xo)
    w_gu = bf(_ffn_interleave(w_up, dff))
    b_gu = row3(_ffn_interleave(b_up, dff))
    dw_gu = _ffn_interleave(w_dw, dff)
    cb_gu = row3(_ffn_interleave(b_dw, dff))
    w_dn = bf(w_down)

    h = _input_ln(x.reshape(m, d), row2(ln_in_w), row2(ln_in_b), tm)
    kv_all = _kvproj(mem.reshape(-1, d), bf(w_xkv))

    for l in range(depth):
        mq, mv, mo, nq, nk, nv, mk_t, gates = _inproj(h, l, w_main, b_main, wk_t, bk, wg_t, bg,
                                                     tm, ml_width)
        hf, hb = _mlstm(mq, mk_t, mv, gates, batch, ML_HEADS)
        na = _na(nq, nk, nv, l, bias_tab, batch, NA_HEADS)
        h = _outproj(h, hf, hb, mo, na, l, row3(ml_norm_w), w_out, row3(b_mix_out),
                     row3(ln1_w), row3(ln1_b), tm, ML_HEADS, alpha)
        h = _xattn(h, l, kv_all, w_q, w_o, row3(b_xo), row3(ln2_w), row3(ln2_b),
                   batch, tm, XA_HEADS, alpha)
        h = _ffn(h, l, w_gu, b_gu, dw_gu, cb_gu, w_dn, row3(b_down), row3(ln3_w), row3(ln3_b),
                 batch, tm, alpha)
    return h.reshape(batch, t, d)
```

```python
import functools

import jax
import jax.numpy as jnp
from jax import lax
from jax.experimental import pallas as pl
from jax.experimental.pallas import tpu as pltpu

F32 = jnp.float32
BF16 = jnp.bfloat16

LN_EPS = 1e-5
GRID_W = 64
ML_HEADS = 4
ML_CHUNK = 128
NA_HEADS = 8
NA_ROWS = 8
NA_COLS = 16
XA_HEADS = 4

VMEM_LIMIT_BYTES = 56 * 1024 * 1024

_NT = (((1,), (1,)), ((), ()))
_TN = (((0,), (0,)), ((), ()))


def _params(*sem):
    return pltpu.CompilerParams(dimension_semantics=sem, vmem_limit_bytes=VMEM_LIMIT_BYTES)


def _layer_norm(x, w, b):
    mu = jnp.mean(x, axis=-1, keepdims=True)
    xc = x - mu
    var = jnp.mean(xc * xc, axis=-1, keepdims=True)
    return xc * lax.rsqrt(var + LN_EPS) * w + b


def _ln_kernel(x_ref, w_ref, b_ref, o_ref):
    o_ref[...] = _layer_norm(x_ref[...], w_ref[...], b_ref[...])


def _input_ln(x, w, b, tm):
    m, d = x.shape
    return pl.pallas_call(
        _ln_kernel,
        grid=(m // tm,),
        in_specs=[pl.BlockSpec((tm, d), lambda i: (i, 0)),
                  pl.BlockSpec((1, d), lambda i: (0, 0)),
                  pl.BlockSpec((1, d), lambda i: (0, 0))],
        out_specs=pl.BlockSpec((tm, d), lambda i: (i, 0)),
        out_shape=jax.ShapeDtypeStruct((m, d), F32),
        compiler_params=_params("parallel"),
        name="input_ln",
    )(x, w, b)


def _scan_chunks(x, reverse, op, fill):
    n = x.shape[-1]
    pos = lax.broadcasted_iota(jnp.int32, x.shape, 1) % ML_CHUNK
    k = 1
    while k < ML_CHUNK:
        if reverse:
            x = op(x, jnp.where(pos < ML_CHUNK - k, pltpu.roll(x, n - k, axis=1), fill))
        else:
            x = op(x, jnp.where(pos >= k, pltpu.roll(x, k, axis=1), fill))
        k *= 2
    return x


def _inproj_kernel(h_ref, w_ref, b_ref, wk_ref, bk_ref, wg_ref, bg_ref,
                   mq_ref, mv_ref, mo_ref, nq_ref, nk_ref, nv_ref, kt_ref, g_ref, *, width):
    hb = h_ref[...].astype(BF16)
    g = lax.dot_general(wg_ref[...], hb, _NT, preferred_element_type=F32) + bg_ref[...]
    heads = g.shape[0] // 4
    rows = []
    for d in range(2):
        ig = g[2 * d * heads:(2 * d + 1) * heads, :]
        lf = jax.nn.log_sigmoid(g[(2 * d + 1) * heads:(2 * d + 2) * heads, :])
        a = _scan_chunks(lf, d == 1, jnp.add, 0.0)
        b = ig - a
        rows += [a, b, _scan_chunks(b, d == 1, jnp.maximum, -jnp.inf)]
    g_ref[...] = jnp.concatenate(rows, axis=0)
    outs = (mq_ref, mv_ref, mo_ref, nq_ref, nk_ref, nv_ref)
    for j, o_ref in enumerate(outs):
        cols = slice(j * width, (j + 1) * width)
        acc = jnp.dot(hb, w_ref[:, cols], preferred_element_type=F32) + b_ref[:, cols]
        o_ref[...] = acc.astype(o_ref.dtype)
    kt = lax.dot_general(wk_ref[...], hb, _NT, preferred_element_type=F32) + bk_ref[...]
    kt_ref[...] = kt.astype(kt_ref.dtype)


def _inproj(h, l, w_main, b_main, wk_t, bk, wg_t, bg, tm, width):
    m, d = h.shape
    ncol = w_main.shape[2]
    ng = wg_t.shape[1]
    ns = 6 * (ng // 4)
    row = lambda i: (i, 0)
    col = lambda i: (0, i)
    layer = lambda i: (l, 0, 0)
    seg = lambda dt: jax.ShapeDtypeStruct((m, width), dt)
    seg_spec = pl.BlockSpec((tm, width), row)
    return pl.pallas_call(
        functools.partial(_inproj_kernel, width=width),
        grid=(m // tm,),
        in_specs=[pl.BlockSpec((tm, d), row),
                  pl.BlockSpec((None, d, ncol), layer),
                  pl.BlockSpec((None, 1, ncol), layer),
                  pl.BlockSpec((None, width, d), layer),
                  pl.BlockSpec((None, width, 1), layer),
                  pl.BlockSpec((None, ng, d), layer),
                  pl.BlockSpec((None, ng, 1), layer)],
        out_specs=[seg_spec] * 6 + [pl.BlockSpec((width, tm), col), pl.BlockSpec((ns, tm), col)],
        out_shape=[seg(BF16)] * 6 + [
                   jax.ShapeDtypeStruct((width, m), BF16), jax.ShapeDtypeStruct((ns, m), F32)],
        compiler_params=_params("parallel"),
        name="mixer_inproj",
    )(h, w_main, b_main, wk_t, bk, wg_t, bg)


def _mlstm_kernel(qf_ref, ktf_ref, vf_ref, gf_ref, qb_ref, ktb_ref, vb_ref, gb_ref,
                  hf_ref, hb_ref, c_ref, m_ref, *, heads, dh):
    L = ML_CHUNK
    scale = dh ** -0.5

    @pl.when(pl.program_id(1) == 0)
    def _():
        c_ref[...] = jnp.zeros_like(c_ref)
        m_ref[...] = jnp.zeros_like(m_ref)

    t_idx = lax.broadcasted_iota(jnp.int32, (L, L), 0)
    s_idx = lax.broadcasted_iota(jnp.int32, (L, L), 1)

    def col_replicated(x_row):
        return jnp.broadcast_to(x_row, (L, L)).T

    ones = jnp.ones((L, dh), BF16)
    probs = []
    for d, (q_ref, kt_ref, v_ref, g_ref, o_ref) in enumerate(
            ((qf_ref, ktf_ref, vf_ref, gf_ref, hf_ref), (qb_ref, ktb_ref, vb_ref, gb_ref, hb_ref))):
        reverse = d == 1
        r0 = 3 * d * heads
        a = g_ref[r0:r0 + heads, :]
        b = g_ref[r0 + heads:r0 + 2 * heads, :]
        b_run = g_ref[r0 + 2 * heads:r0 + 3 * heads, :]
        m_prev = m_ref[d]
        g_run = jnp.maximum(m_prev, b_run)
        a_end = a[:, 0:1] if reverse else a[:, L - 1:L]
        up_log = a_end + b
        m_new = jnp.maximum(a_end + m_prev, jnp.max(up_log, axis=1, keepdims=True))
        wk = jnp.exp(up_log - m_new) * scale
        decay = jnp.exp(a_end + m_prev - m_new)
        m_ref[d] = m_new
        causal = (s_idx >= t_idx) if reverse else (s_idx <= t_idx)
        for hd in range(heads):
            row = slice(hd, hd + 1)
            probs.append(dict(
                st=d * heads + hd, cols=slice(hd * dh, (hd + 1) * dh), causal=causal,
                q_ref=q_ref, kt_ref=kt_ref, v_ref=v_ref, o_ref=o_ref,
                a=a[row], b=b[row], g=g_run[row], m_prev=m_prev[row], wk=wk[row], decay=decay[row]))

    for p in probs:
        p["a_cb"] = col_replicated(p["a"])
        p["g_cb"] = col_replicated(p["g"])
    for p in probs:
        p["q"] = p["q_ref"][:, p["cols"]]
        p["s"] = jnp.dot(p["q"], p["kt_ref"][p["cols"], :], preferred_element_type=F32)
    for p in probs:
        w = jnp.where(p["causal"], jnp.exp(p["b"] - p["g_cb"]), 0.0)
        p["w_qk"] = (w * (p["s"] * scale)).astype(BF16)
        p["inter"] = jnp.exp(p["m_prev"] - p["g_cb"])
        p["floor"] = jnp.exp(-(p["a_cb"] + p["g_cb"]))
    for p in probs:
        p["v_aug"] = jnp.concatenate([p["v_ref"][:, p["cols"]], ones], axis=1)
        p["c_aug"] = c_ref[p["st"]]
        q_c = jnp.dot(p["q"], p["c_aug"].astype(BF16), preferred_element_type=F32)
        w_v = jnp.dot(p["w_qk"], p["v_aug"], preferred_element_type=F32)
        num = w_v[:, :dh] + p["inter"] * q_c[:, :dh]
        den = w_v[:, dh:] + p["inter"] * q_c[:, dh:]
        h_out = num / jnp.maximum(jnp.abs(den), p["floor"])
        p["o_ref"][:, p["cols"]] = h_out.astype(p["o_ref"].dtype)
    for p in probs:
        kw_t = (p["kt_ref"][p["cols"], :].astype(F32) * p["wk"]).astype(BF16)
        decay2 = jnp.concatenate([p["decay"], p["decay"]], axis=1)
        c_ref[p["st"]] = decay2 * p["c_aug"] + jnp.dot(kw_t, p["v_aug"], preferred_element_type=F32)


def _mlstm(mq, mk_t, mv, gates, batch, heads):
    m, width = mq.shape
    dh = width // heads
    nc = m // batch // ML_CHUNK
    ng = gates.shape[0]
    fwd = lambda b, c: (b * nc + c, 0)
    bwd = lambda b, c: (b * nc + nc - 1 - c, 0)
    fwd_t = lambda b, c: (0, b * nc + c)
    bwd_t = lambda b, c: (0, b * nc + nc - 1 - c)
    blk = lambda im: pl.BlockSpec((ML_CHUNK, width), im)
    tblk = lambda im: pl.BlockSpec((width, ML_CHUNK), im)
    gblk = lambda im: pl.BlockSpec((ng, ML_CHUNK), im)
    return pl.pallas_call(
        functools.partial(_mlstm_kernel, heads=heads, dh=dh),
        grid=(batch, nc),
        in_specs=[blk(fwd), tblk(fwd_t), blk(fwd), gblk(fwd_t),
                  blk(bwd), tblk(bwd_t), blk(bwd), gblk(bwd_t)],
        out_specs=[blk(fwd), blk(bwd)],
        out_shape=[jax.ShapeDtypeStruct((m, width), BF16)] * 2,
        scratch_shapes=[pltpu.VMEM((2 * heads, dh, 2 * dh), F32),
                        pltpu.VMEM((2, heads, ML_CHUNK), F32)],
        compiler_params=_params("parallel", "arbitrary"),
        name="mlstm_scan",
    )(mq, mk_t, mv, gates, mq, mk_t, mv, gates)


def _na_bias_table(rpb):
    j = jnp.arange(GRID_W)
    dc = jnp.clip(j[None, :] - j[:, None] + NA_COLS - 1, 0, 2 * NA_COLS - 2)
    cs = jnp.clip(j - NA_COLS // 2, 0, GRID_W - NA_COLS)
    ok = (j[None, :] >= cs[:, None]) & (j[None, :] < cs[:, None] + NA_COLS)
    onehot = (dc[..., None] == jnp.arange(2 * NA_COLS - 1)).astype(F32)
    cols = jnp.einsum('hdc,qkc->hdqk', rpb.astype(F32), onehot, precision=lax.Precision.HIGHEST)
    cols = jnp.where(ok[None, None], cols, -jnp.inf)
    return jnp.concatenate([cols[:, :-1], cols[:, 1:]], axis=-1)


def _na_window_start(r, rows):
    return jnp.clip(r - NA_ROWS // 2, 0, rows - NA_ROWS)


NA_ROWS_PER_STEP = 4


def _na_kernel(q_ref, k_ref, v_ref, bias_ref, o_ref, *, rows, heads, dh):
    nkeys = NA_ROWS * GRID_W
    scale = dh ** -0.5
    pw = 2 * dh
    lane_lo = lax.broadcasted_iota(jnp.int32, (GRID_W, pw), 1) < dh
    ones = jnp.ones((nkeys, pw), BF16)
    probs = []
    for rr in range(NA_ROWS_PER_STEP):
        r = pl.program_id(1) * NA_ROWS_PER_STEP + rr
        start = _na_window_start(r, rows)
        tok0 = pl.multiple_of(start * GRID_W, GRID_W)
        off = NA_ROWS - 1 - (r - start)
        for hp in range(heads // 2):
            probs.append((rr, hp, tok0, off))
    s = []
    for rr, hp, tok0, off in probs:
        pcols = slice(hp * pw, (hp + 1) * pw)
        q = q_ref[rr * GRID_W:(rr + 1) * GRID_W, pcols]
        k = k_ref[pl.ds(tok0, nkeys), pcols]
        for half in range(2):
            qh = jnp.where(lane_lo if half == 0 else jnp.logical_not(lane_lo), q, jnp.zeros_like(q))
            bias = jnp.concatenate([bias_ref[2 * hp + half, off + 2 * jj]
                                    for jj in range(NA_ROWS // 2)], axis=1)
            s.append(lax.dot_general(qh, k, _NT, preferred_element_type=F32) * scale + bias)
    mx = [jnp.max(x, axis=1, keepdims=True) for x in s]
    p = [jnp.exp(x - m).astype(BF16) for x, m in zip(s, mx)]
    outs = []
    for i, (rr, hp, tok0, off) in enumerate(probs):
        v_aug = jnp.concatenate([v_ref[pl.ds(tok0, nkeys), hp * pw:(hp + 1) * pw], ones], axis=1)
        r_lo = jnp.dot(p[2 * i], v_aug, preferred_element_type=F32)
        r_hi = jnp.dot(p[2 * i + 1], v_aug, preferred_element_type=F32)
        both = jnp.where(jnp.concatenate([lane_lo, lane_lo], axis=1), r_lo, r_hi)
        outs.append(both[:, :pw] / both[:, pw:])
    npair = heads // 2
    for rr in range(NA_ROWS_PER_STEP):
        o_ref[rr * GRID_W:(rr + 1) * GRID_W, :] = jnp.concatenate(
            outs[rr * npair:(rr + 1) * npair], axis=1).astype(o_ref.dtype)


def _na(nq, nk, nv, l, bias_tab, batch, heads):
    m, width = nq.shape
    t = m // batch
    rows = t // GRID_W
    dh = width // heads
    nkeys = NA_ROWS * GRID_W
    rps = NA_ROWS_PER_STEP
    steps = rows // rps
    qblk = pl.BlockSpec((rps * GRID_W, width), lambda b, r: (b * steps + r, 0))
    return pl.pallas_call(
        functools.partial(_na_kernel, rows=rows, heads=heads, dh=dh),
        grid=(batch, steps),
        in_specs=[qblk,
                  pl.BlockSpec((t, width), lambda b, r: (b, 0)),
                  pl.BlockSpec((t, width), lambda b, r: (b, 0)),
                  pl.BlockSpec((None, heads, 2 * NA_ROWS - 2, GRID_W, 2 * GRID_W),
                               lambda b, r: (l, 0, 0, 0, 0))],
        out_specs=qblk,
        out_shape=jax.ShapeDtypeStruct((m, width), BF16),
        compiler_params=_params("parallel", "arbitrary"),
        name="neighborhood_attn",
    )(nq, nk, nv, bias_tab)


def _kvproj_kernel(mem_ref, w_ref, o_ref):
    o_ref[0] = jnp.dot(mem_ref[...].astype(BF16), w_ref[0],
                       preferred_element_type=F32).astype(o_ref.dtype)


def _kvproj(mem, w_kv):
    mm, d = mem.shape
    depth, _, n = w_kv.shape
    return pl.pallas_call(
        _kvproj_kernel,
        grid=(depth,),
        in_specs=[pl.BlockSpec((mm, d), lambda l: (0, 0)),
                  pl.BlockSpec((1, d, n), lambda l: (l, 0, 0))],
        out_specs=pl.BlockSpec((1, mm, n), lambda l: (l, 0, 0)),
        out_shape=jax.ShapeDtypeStruct((depth, mm, n), BF16),
        compiler_params=_params("parallel"),
        name="memory_kv_proj",
    )(mem, w_kv)


XA_ROW_SPLIT = 2


def _mix_xattn_kernel(h_ref, hf_ref, hb_ref, mo_ref, na_ref, nw_ref, w_ref, b_ref, l1w_ref, l1b_ref,
                      kv_ref, wq_ref, wo_ref, bo_ref, l2w_ref, l2b_ref, o_ref,
                      *, ml_heads, xa_heads, alpha):
    d = h_ref.shape[1]
    width = hf_ref.shape[1]
    mdh = width // ml_heads
    dh = d // xa_heads
    scale = dh ** -0.5
    rg = h_ref.shape[0] // XA_ROW_SPLIT
    groups = [slice(r * rg, (r + 1) * rg) for r in range(XA_ROW_SPLIT)]

    def mixer_out(rows):
        hm = hf_ref[rows, :].astype(F32) + hb_ref[rows, :].astype(F32)
        parts = []
        for hd in range(ml_heads):
            x = hm[:, hd * mdh:(hd + 1) * mdh]
            mu = jnp.mean(x, axis=-1, keepdims=True)
            xc = x - mu
            var = jnp.mean(xc * xc, axis=-1, keepdims=True)
            parts.append(xc * lax.rsqrt(var + LN_EPS))
        hn = jnp.concatenate(parts, axis=1)
        ml = jax.nn.sigmoid(mo_ref[rows, :].astype(F32)) * (hn * nw_ref[...])
        y = (jnp.dot(ml.astype(BF16), w_ref[0:width, :], preferred_element_type=F32)
             + jnp.dot(na_ref[rows, :], w_ref[width:, :], preferred_element_type=F32) + b_ref[...])
        return _layer_norm(alpha * h_ref[rows, :] + y, l1w_ref[...], l1b_ref[...])

    h1 = [mixer_out(rows) for rows in groups]
    q = [jnp.dot(x.astype(BF16), wq_ref[...], preferred_element_type=F32).astype(BF16) for x in h1]
    s = [[lax.dot_general(qr[:, hd * dh:(hd + 1) * dh], kv_ref[:, hd * dh:(hd + 1) * dh], _NT,
                          preferred_element_type=F32) * scale for hd in range(xa_heads)] for qr in q]
    p = [[jnp.exp(x - jnp.max(x, axis=1, keepdims=True)) for x in sr] for sr in s]
    o = [jnp.concatenate(
        [(jnp.dot(x.astype(BF16), kv_ref[:, d + hd * dh:d + (hd + 1) * dh],
                  preferred_element_type=F32) / jnp.sum(x, axis=1, keepdims=True)).astype(BF16)
         for hd, x in enumerate(pr)], axis=1) for pr in p]
    y = [jnp.dot(orow, wo_ref[...], preferred_element_type=F32) + bo_ref[...] for orow in o]
    for rows, x, yr in zip(groups, h1, y):
        o_ref[rows, :] = _layer_norm(alpha * x + yr, l2w_ref[...], l2b_ref[...])


def _mix_xattn(h, hf, hb, mo, na, l, nw, w, b, l1w, l1b, kv, wq, wo, bo, l2w, l2b,
               batch, tm, ml_heads, xa_heads, alpha):
    m, d = h.shape
    width = hf.shape[1]
    nt = m // batch // tm
    mem_len = kv.shape[1] // batch
    row = lambda bi, i: (bi * nt + i, 0)
    layer = lambda bi, i: (l, 0, 0)
    vec = pl.BlockSpec((None, 1, d), layer)
    mat = pl.BlockSpec((None, d, d), layer)
    return pl.pallas_call(
        functools.partial(_mix_xattn_kernel, ml_heads=ml_heads, xa_heads=xa_heads, alpha=alpha),
        grid=(batch, nt),
        in_specs=[pl.BlockSpec((tm, d), row)] + [pl.BlockSpec((tm, width), row)] * 4
                 + [pl.BlockSpec((None, 1, width), layer), mat, vec, vec, vec,
                    pl.BlockSpec((None, mem_len, 2 * d), lambda bi, i: (l, bi, 0)),
                    mat, mat, vec, vec, vec],
        out_specs=pl.BlockSpec((tm, d), row),
        out_shape=jax.ShapeDtypeStruct((m, d), F32),
        compiler_params=_params("parallel", "parallel"),
        name="mixer_out_xattn_ln",
    )(h, hf, hb, mo, na, nw, w, b, l1w, l1b, kv, wq, wo, bo, l2w, l2b)


FFN_HALO = 16
FFN_SUB = 256


def _ffn_kernel(h_ref, hp_ref, hn_ref, w_ref, b_ref, dw_ref, cb_ref, wd_ref, bd_ref, lw_ref, lb_ref,
                o_ref, hb_ref, x_ref, y_ref, *, alpha):
    i = pl.program_id(1)
    tm = h_ref.shape[0]
    hal = FFN_HALO
    sb = FFN_SUB
    dff = y_ref.shape[1]
    nsub = dff // sb
    first = i == 0
    last = i == pl.num_programs(1) - 1

    hb_ref[0:hal, :] = hp_ref[...].astype(BF16)
    hb_ref[hal:hal + tm, :] = h_ref[...].astype(BF16)
    hb_ref[hal + tm:, :] = hn_ref[...].astype(BF16)

    def sides(c):
        return ((slice(0, sb), slice(c * sb, (c + 1) * sb)),
                (slice(sb, 2 * sb), slice(dff + c * sb, dff + (c + 1) * sb)))

    def project(c):
        for lanes, cols in sides(c):
            x = jnp.dot(hb_ref[...], w_ref[:, cols], preferred_element_type=F32)
            nb = -b_ref[:, cols]
            x_ref[c % 2, 0:hal, lanes] = jnp.where(first, nb, x[0:hal])
            x_ref[c % 2, hal:hal + tm, lanes] = x[hal:hal + tm]
            x_ref[c % 2, hal + tm:, lanes] = jnp.where(last, nb, x[hal + tm:])

    def gate(c):
        xs = x_ref.at[c % 2]
        conv = []
        for lanes, cols in sides(c):
            taps = [dw_ref[k:k + 1, cols] for k in range(3)]
            bias = b_ref[:, cols] * (taps[0] + taps[1] + taps[2]) + cb_ref[:, cols]
            conv.append(taps[0] * xs[hal - 1:hal - 1 + tm, lanes] + taps[1] * xs[hal:hal + tm, lanes]
                        + taps[2] * xs[hal + 1:hal + 1 + tm, lanes] + bias)
        g, u = conv
        y = (0.5 * g * (1.0 + lax.erf(g * (2.0 ** -0.5)))) * u
        y_ref[:, c * sb:(c + 1) * sb] = y.astype(BF16)

    project(0)
    for c in range(nsub):
        if c + 1 < nsub:
            project(c + 1)
        gate(c)
    down = jnp.dot(y_ref[...], wd_ref[...], preferred_element_type=F32)
    o_ref[...] = _layer_norm(alpha * h_ref[...] + down + bd_ref[...], lw_ref[...], lb_ref[...])


def _ffn(h, l, w_up, b_up, w_dw, b_dw, w_down, b_down, lw, lb, batch, tm, alpha):
    m, d = h.shape
    dff = w_down.shape[1]
    t = m // batch
    nt = t // tm
    hal = FFN_HALO
    row = lambda b, i: (b * nt + i, 0)
    prev = lambda b, i: (jnp.maximum((b * t + i * tm) // hal - 1, 0), 0)
    nxt = lambda b, i: (jnp.minimum((b * t + (i + 1) * tm) // hal, m // hal - 1), 0)
    layer = lambda b, i: (l, 0, 0)
    resident = dict(pipeline_mode=pl.Buffered(1))
    return pl.pallas_call(
        functools.partial(_ffn_kernel, alpha=alpha),
        grid=(batch, nt),
        in_specs=[pl.BlockSpec((tm, d), row), pl.BlockSpec((hal, d), prev), pl.BlockSpec((hal, d), nxt),
                  pl.BlockSpec((None, d, 2 * dff), layer, **resident),
                  pl.BlockSpec((None, 1, 2 * dff), layer),
                  pl.BlockSpec((None, 3, 2 * dff), layer),
                  pl.BlockSpec((None, 1, 2 * dff), layer),
                  pl.BlockSpec((None, dff, d), layer, **resident)]
                 + [pl.BlockSpec((None, 1, d), layer)] * 3,
        out_specs=pl.BlockSpec((tm, d), row),
        out_shape=jax.ShapeDtypeStruct((m, d), F32),
        scratch_shapes=[pltpu.VMEM((tm + 2 * hal, d), BF16),
                        pltpu.VMEM((2, tm + 2 * hal, 2 * FFN_SUB), F32),
                        pltpu.VMEM((tm, dff), BF16)],
        compiler_params=_params("parallel", "arbitrary"),
        name="conv_ffn_ln",
    )(h, h, h, w_up, b_up, w_dw, b_dw, w_down, b_down, lw, lb)


def _pick(n, prefs):
    for p in prefs:
        if n % p == 0:
            return p
    return n


def kernel(x, mem, ln_in_w, ln_in_b, w_in, b_in, ml_norm_w, na_rpb, w_mix_out, b_mix_out,
           ln1_w, ln1_b, w_xq, w_xkv, w_xo, b_xo, ln2_w, ln2_b,
           w_up, b_up, w_dw, b_dw, w_down, b_down, ln3_w, ln3_b):
    batch, t, d = x.shape
    depth = w_in.shape[0]
    m = batch * t
    ml_width = d // 2
    na_width = d - ml_width
    dff = w_down.shape[1]
    alpha = (2.0 * depth) ** 0.25
    assert ml_width == na_width and t % (GRID_W * NA_ROWS) == 0 and t % ML_CHUNK == 0

    tm = _pick(t, (512, 256, 128))

    row2 = lambda a: a.reshape(1, -1)
    row3 = lambda a: a[:, None, :]
    col3 = lambda a: a[:, :, None]
    bf = lambda a: a.astype(BF16)
    k0, k1 = ml_width, 2 * ml_width
    g0 = 4 * ml_width
    g1 = g0 + 4 * ML_HEADS
    w_main = bf(jnp.concatenate([w_in[:, :, :k0], w_in[:, :, k1:g0], w_in[:, :, g1:]], axis=2))
    b_main = row3(jnp.concatenate([b_in[:, :k0], b_in[:, k1:g0], b_in[:, g1:]], axis=1))
    wk_t = bf(jnp.swapaxes(w_in[:, :, k0:k1], 1, 2))
    bk = col3(b_in[:, k0:k1])
    wg_t = bf(jnp.swapaxes(w_in[:, :, g0:g1], 1, 2))
    bg = col3(b_in[:, g0:g1])
    bias_tab = jax.vmap(_na_bias_table)(na_rpb)
    w_out, w_q, w_o = bf(w_mix_out), bf(w_xq), bf(w_xo)
    w_upb, w_dn = bf(w_up), bf(w_down)

    h = _input_ln(x.reshape(m, d), row2(ln_in_w), row2(ln_in_b), tm)
    kv_all = _kvproj(mem.reshape(-1, d), bf(w_xkv))

    for l in range(depth):
        mq, mv, mo, nq, nk, nv, mk_t, gates = _inproj(h, l, w_main, b_main, wk_t, bk, wg_t, bg,
                                                     tm, ml_width)
        hf, hb = _mlstm(mq, mk_t, mv, gates, batch, ML_HEADS)
        na = _na(nq, nk, nv, l, bias_tab, batch, NA_HEADS)
        h = _mix_xattn(h, hf, hb, mo, na, l, row3(ml_norm_w), w_out, row3(b_mix_out),
                       row3(ln1_w), row3(ln1_b), kv_all, w_q, w_o, row3(b_xo),
                       row3(ln2_w), row3(ln2_b), batch, tm, ML_HEADS, XA_HEADS, alpha)
        h = _ffn(h, l, w_upb, row3(b_up), w_dw, row3(b_dw), w_dn, row3(b_down), row3(ln3_w), row3(ln3_b),
                 batch, tm, alpha)
    return h.reshape(batch, t, d)
```

```python
import functools

import jax
import jax.numpy as jnp
from jax import lax
from jax.experimental import pallas as pl
from jax.experimental.pallas import tpu as pltpu

F32 = jnp.float32
BF16 = jnp.bfloat16

LN_EPS = 1e-5
GRID_W = 64
ML_HEADS = 4
ML_CHUNK = 128
NA_HEADS = 8
NA_ROWS = 8
NA_COLS = 16
XA_HEADS = 4

VMEM_LIMIT_BYTES = 56 * 1024 * 1024

_NT = (((1,), (1,)), ((), ()))
_TN = (((0,), (0,)), ((), ()))


def _params(*sem):
    return pltpu.CompilerParams(dimension_semantics=sem, vmem_limit_bytes=VMEM_LIMIT_BYTES)


def _layer_norm(x, w, b):
    mu = jnp.mean(x, axis=-1, keepdims=True)
    xc = x - mu
    var = jnp.mean(xc * xc, axis=-1, keepdims=True)
    return xc * lax.rsqrt(var + LN_EPS) * w + b


def _ln_kernel(x_ref, w_ref, b_ref, o_ref):
    o_ref[...] = _layer_norm(x_ref[...], w_ref[...], b_ref[...])


def _input_ln(x, w, b, tm):
    m, d = x.shape
    return pl.pallas_call(
        _ln_kernel,
        grid=(m // tm,),
        in_specs=[pl.BlockSpec((tm, d), lambda i: (i, 0)),
                  pl.BlockSpec((1, d), lambda i: (0, 0)),
                  pl.BlockSpec((1, d), lambda i: (0, 0))],
        out_specs=pl.BlockSpec((tm, d), lambda i: (i, 0)),
        out_shape=jax.ShapeDtypeStruct((m, d), F32),
        compiler_params=_params("parallel"),
        name="input_ln",
    )(x, w, b)


def _scan_chunks(x, reverse, op, fill):
    n = x.shape[-1]
    pos = lax.broadcasted_iota(jnp.int32, x.shape, 1) % ML_CHUNK
    k = 1
    while k < ML_CHUNK:
        if reverse:
            x = op(x, jnp.where(pos < ML_CHUNK - k, pltpu.roll(x, n - k, axis=1), fill))
        else:
            x = op(x, jnp.where(pos >= k, pltpu.roll(x, k, axis=1), fill))
        k *= 2
    return x


def _inproj_kernel(h_ref, w_ref, b_ref, wk_ref, bk_ref, wg_ref, bg_ref,
                   mq_ref, mv_ref, mo_ref, nq_ref, nk_ref, nv_ref, kt_ref, g_ref, *, width):
    hb = h_ref[...].astype(BF16)
    g = lax.dot_general(wg_ref[...], hb, _NT, preferred_element_type=F32) + bg_ref[...]
    heads = g.shape[0] // 4
    rows = []
    for d in range(2):
        ig = g[2 * d * heads:(2 * d + 1) * heads, :]
        lf = jax.nn.log_sigmoid(g[(2 * d + 1) * heads:(2 * d + 2) * heads, :])
        a = _scan_chunks(lf, d == 1, jnp.add, 0.0)
        b = ig - a
        rows += [a, b, _scan_chunks(b, d == 1, jnp.maximum, -jnp.inf)]
    g_ref[...] = jnp.concatenate(rows, axis=0)
    outs = (mq_ref, mv_ref, mo_ref, nq_ref, nk_ref, nv_ref)
    for j, o_ref in enumerate(outs):
        cols = slice(j * width, (j + 1) * width)
        acc = jnp.dot(hb, w_ref[:, cols], preferred_element_type=F32) + b_ref[:, cols]
        o_ref[...] = acc.astype(o_ref.dtype)
    kt = lax.dot_general(wk_ref[...], hb, _NT, preferred_element_type=F32) + bk_ref[...]
    kt_ref[...] = kt.astype(kt_ref.dtype)


def _inproj(h, l, w_main, b_main, wk_t, bk, wg_t, bg, tm, width):
    m, d = h.shape
    ncol = w_main.shape[2]
    ng = wg_t.shape[1]
    ns = 6 * (ng // 4)
    row = lambda i: (i, 0)
    col = lambda i: (0, i)
    layer = lambda i: (l, 0, 0)
    seg = lambda dt: jax.ShapeDtypeStruct((m, width), dt)
    seg_spec = pl.BlockSpec((tm, width), row)
    return pl.pallas_call(
        functools.partial(_inproj_kernel, width=width),
        grid=(m // tm,),
        in_specs=[pl.BlockSpec((tm, d), row),
                  pl.BlockSpec((None, d, ncol), layer),
                  pl.BlockSpec((None, 1, ncol), layer),
                  pl.BlockSpec((None, width, d), layer),
                  pl.BlockSpec((None, width, 1), layer),
                  pl.BlockSpec((None, ng, d), layer),
                  pl.BlockSpec((None, ng, 1), layer)],
        out_specs=[seg_spec] * 6 + [pl.BlockSpec((width, tm), col), pl.BlockSpec((ns, tm), col)],
        out_shape=[seg(BF16)] * 6 + [
                   jax.ShapeDtypeStruct((width, m), BF16), jax.ShapeDtypeStruct((ns, m), F32)],
        compiler_params=_params("parallel"),
        name="mixer_inproj",
    )(h, w_main, b_main, wk_t, bk, wg_t, bg)


def _mlstm_kernel(qf_ref, ktf_ref, vf_ref, gf_ref, qb_ref, ktb_ref, vb_ref, gb_ref,
                  hf_ref, hb_ref, c_ref, m_ref, *, heads, dh):
    L = ML_CHUNK
    scale = dh ** -0.5

    @pl.when(pl.program_id(1) == 0)
    def _():
        c_ref[...] = jnp.zeros_like(c_ref)
        m_ref[...] = jnp.zeros_like(m_ref)

    t_idx = lax.broadcasted_iota(jnp.int32, (L, L), 0)
    s_idx = lax.broadcasted_iota(jnp.int32, (L, L), 1)

    def col_replicated(x_row):
        return jnp.broadcast_to(x_row, (L, L)).T

    ones = jnp.ones((L, dh), BF16)
    probs = []
    for d, (q_ref, kt_ref, v_ref, g_ref, o_ref) in enumerate(
            ((qf_ref, ktf_ref, vf_ref, gf_ref, hf_ref), (qb_ref, ktb_ref, vb_ref, gb_ref, hb_ref))):
        reverse = d == 1
        r0 = 3 * d * heads
        a = g_ref[r0:r0 + heads, :]
        b = g_ref[r0 + heads:r0 + 2 * heads, :]
        b_run = g_ref[r0 + 2 * heads:r0 + 3 * heads, :]
        m_prev = m_ref[d]
        g_run = jnp.maximum(m_prev, b_run)
        a_end = a[:, 0:1] if reverse else a[:, L - 1:L]
        up_log = a_end + b
        m_new = jnp.maximum(a_end + m_prev, jnp.max(up_log, axis=1, keepdims=True))
        wk = jnp.exp(up_log - m_new) * scale
        decay = jnp.exp(a_end + m_prev - m_new)
        m_ref[d] = m_new
        causal = (s_idx >= t_idx) if reverse else (s_idx <= t_idx)
        for hd in range(heads):
            row = slice(hd, hd + 1)
            probs.append(dict(
                st=d * heads + hd, cols=slice(hd * dh, (hd + 1) * dh), causal=causal,
                q_ref=q_ref, kt_ref=kt_ref, v_ref=v_ref, o_ref=o_ref,
                a=a[row], b=b[row], g=g_run[row], m_prev=m_prev[row], wk=wk[row], decay=decay[row]))

    for p in probs:
        p["a_cb"] = col_replicated(p["a"])
        p["g_cb"] = col_replicated(p["g"])
    for p in probs:
        p["q"] = p["q_ref"][:, p["cols"]]
        p["s"] = jnp.dot(p["q"], p["kt_ref"][p["cols"], :], preferred_element_type=F32)
    for p in probs:
        w = jnp.where(p["causal"], jnp.exp(p["b"] - p["g_cb"]), 0.0)
        p["w_qk"] = (w * (p["s"] * scale)).astype(BF16)
        p["inter"] = jnp.exp(p["m_prev"] - p["g_cb"])
        p["floor"] = jnp.exp(-(p["a_cb"] + p["g_cb"]))
    for p in probs:
        p["v_aug"] = jnp.concatenate([p["v_ref"][:, p["cols"]], ones], axis=1)
        p["c_aug"] = c_ref[p["st"]]
        q_c = jnp.dot(p["q"], p["c_aug"].astype(BF16), preferred_element_type=F32)
        w_v = jnp.dot(p["w_qk"], p["v_aug"], preferred_element_type=F32)
        num = w_v[:, :dh] + p["inter"] * q_c[:, :dh]
        den = w_v[:, dh:] + p["inter"] * q_c[:, dh:]
        h_out = num / jnp.maximum(jnp.abs(den), p["floor"])
        p["o_ref"][:, p["cols"]] = h_out.astype(p["o_ref"].dtype)
    for p in probs:
        kw_t = (p["kt_ref"][p["cols"], :].astype(F32) * p["wk"]).astype(BF16)
        decay2 = jnp.concatenate([p["decay"], p["decay"]], axis=1)
        c_ref[p["st"]] = decay2 * p["c_aug"] + jnp.dot(kw_t, p["v_aug"], preferred_element_type=F32)


def _mlstm(mq, mk_t, mv, gates, batch, heads):
    m, width = mq.shape
    dh = width // heads
    nc = m // batch // ML_CHUNK
    ng = gates.shape[0]
    fwd = lambda b, c: (b * nc + c, 0)
    bwd = lambda b, c: (b * nc + nc - 1 - c, 0)
    fwd_t = lambda b, c: (0, b * nc + c)
    bwd_t = lambda b, c: (0, b * nc + nc - 1 - c)
    blk = lambda im: pl.BlockSpec((ML_CHUNK, width), im)
    tblk = lambda im: pl.BlockSpec((width, ML_CHUNK), im)
    gblk = lambda im: pl.BlockSpec((ng, ML_CHUNK), im)
    return pl.pallas_call(
        functools.partial(_mlstm_kernel, heads=heads, dh=dh),
        grid=(batch, nc),
        in_specs=[blk(fwd), tblk(fwd_t), blk(fwd), gblk(fwd_t),
                  blk(bwd), tblk(bwd_t), blk(bwd), gblk(bwd_t)],
        out_specs=[blk(fwd), blk(bwd)],
        out_shape=[jax.ShapeDtypeStruct((m, width), BF16)] * 2,
        scratch_shapes=[pltpu.VMEM((2 * heads, dh, 2 * dh), F32),
                        pltpu.VMEM((2, heads, ML_CHUNK), F32)],
        compiler_params=_params("parallel", "arbitrary"),
        name="mlstm_scan",
    )(mq, mk_t, mv, gates, mq, mk_t, mv, gates)


def _na_bias_table(rpb):
    j = jnp.arange(GRID_W)
    dc = jnp.clip(j[None, :] - j[:, None] + NA_COLS - 1, 0, 2 * NA_COLS - 2)
    cs = jnp.clip(j - NA_COLS // 2, 0, GRID_W - NA_COLS)
    ok = (j[None, :] >= cs[:, None]) & (j[None, :] < cs[:, None] + NA_COLS)
    onehot = (dc[..., None] == jnp.arange(2 * NA_COLS - 1)).astype(F32)
    cols = jnp.einsum('hdc,qkc->hdqk', rpb.astype(F32), onehot, precision=lax.Precision.HIGHEST)
    cols = jnp.where(ok[None, None], cols, -jnp.inf)
    return jnp.concatenate([cols[:, :-1], cols[:, 1:]], axis=-1)


def _na_window_start(r, rows):
    return jnp.clip(r - NA_ROWS // 2, 0, rows - NA_ROWS)


NA_GROUP = 4
NA_KEY_TILES = 3


def _na_kernel(q_ref, k_ref, v_ref, bias_ref, o_ref, *, rows, heads, dh):
    g = pl.program_id(1)
    ktile = NA_GROUP * GRID_W
    nkeys = NA_KEY_TILES * ktile
    scale = dh ** -0.5
    pw = 2 * dh
    npairs = nkeys // (2 * GRID_W)
    jt = jnp.clip(g - 1, 0, rows // NA_GROUP - NA_KEY_TILES)
    krow0 = jt * NA_GROUP
    tok0 = pl.multiple_of(krow0 * GRID_W, ktile)
    lane_lo = lax.broadcasted_iota(jnp.int32, (GRID_W, pw), 1) < dh
    ones = jnp.ones((nkeys, pw), BF16)
    neg = jnp.full((GRID_W, pw), -jnp.inf, F32)
    pair_row = jnp.where(lane_lo, 0, 1)

    def bias_rows(r, head):
        start = _na_window_start(r, rows)
        blocks = []
        for p in range(npairs):
            kr = krow0 + 2 * p
            dr = jnp.clip(kr - r + NA_ROWS - 1, 0, 2 * NA_ROWS - 3)
            krow = kr + pair_row
            ok = jnp.logical_and(krow >= start, krow < start + NA_ROWS)
            blocks.append(jnp.where(ok, bias_ref[head, dr], neg))
        return jnp.concatenate(blocks, axis=1)

    s = []
    for hp in range(heads // 2):
        pcols = slice(hp * pw, (hp + 1) * pw)
        qs, bs = [], []
        for i in range(NA_GROUP):
            q = q_ref[i * GRID_W:(i + 1) * GRID_W, pcols]
            for half in range(2):
                qs.append(jnp.where(lane_lo if half == 0 else jnp.logical_not(lane_lo), q,
                                    jnp.zeros_like(q)))
                bs.append(bias_rows(g * NA_GROUP + i, 2 * hp + half))
        k = k_ref[pl.ds(tok0, nkeys), pcols]
        s.append(lax.dot_general(jnp.concatenate(qs, axis=0), k, _NT,
                                 preferred_element_type=F32) * scale + jnp.concatenate(bs, axis=0))
    p = [jnp.exp(x - jnp.max(x, axis=1, keepdims=True)).astype(BF16) for x in s]
    outs = []
    for hp, x in enumerate(p):
        v_aug = jnp.concatenate([v_ref[pl.ds(tok0, nkeys), hp * pw:(hp + 1) * pw], ones], axis=1)
        res = jnp.dot(x, v_aug, preferred_element_type=F32)
        per_row = []
        for i in range(NA_GROUP):
            lo = res[(2 * i) * GRID_W:(2 * i + 1) * GRID_W]
            hi = res[(2 * i + 1) * GRID_W:(2 * i + 2) * GRID_W]
            both = jnp.where(jnp.concatenate([lane_lo, lane_lo], axis=1), lo, hi)
            per_row.append(both[:, :pw] / both[:, pw:])
        outs.append(jnp.concatenate(per_row, axis=0))
    o_ref[...] = jnp.concatenate(outs, axis=1).astype(o_ref.dtype)


def _na(nq, nk, nv, l, bias_tab, batch, heads):
    m, width = nq.shape
    t = m // batch
    rows = t // GRID_W
    dh = width // heads
    steps = rows // NA_GROUP
    qblk = pl.BlockSpec((NA_GROUP * GRID_W, width), lambda b, r: (b * steps + r, 0))
    return pl.pallas_call(
        functools.partial(_na_kernel, rows=rows, heads=heads, dh=dh),
        grid=(batch, steps),
        in_specs=[qblk,
                  pl.BlockSpec((t, width), lambda b, r: (b, 0)),
                  pl.BlockSpec((t, width), lambda b, r: (b, 0)),
                  pl.BlockSpec((None, heads, 2 * NA_ROWS - 2, GRID_W, 2 * GRID_W),
                               lambda b, r: (l, 0, 0, 0, 0))],
        out_specs=qblk,
        out_shape=jax.ShapeDtypeStruct((m, width), BF16),
        compiler_params=_params("parallel", "arbitrary"),
        name="neighborhood_attn",
    )(nq, nk, nv, bias_tab)


def _kvproj_kernel(mem_ref, w_ref, o_ref):
    o_ref[0] = jnp.dot(mem_ref[...].astype(BF16), w_ref[0],
                       preferred_element_type=F32).astype(o_ref.dtype)


def _kvproj(mem, w_kv):
    mm, d = mem.shape
    depth, _, n = w_kv.shape
    return pl.pallas_call(
        _kvproj_kernel,
        grid=(depth,),
        in_specs=[pl.BlockSpec((mm, d), lambda l: (0, 0)),
                  pl.BlockSpec((1, d, n), lambda l: (l, 0, 0))],
        out_specs=pl.BlockSpec((1, mm, n), lambda l: (l, 0, 0)),
        out_shape=jax.ShapeDtypeStruct((depth, mm, n), BF16),
        compiler_params=_params("parallel"),
        name="memory_kv_proj",
    )(mem, w_kv)


XA_ROW_SPLIT = 2


def _mix_xattn_kernel(h_ref, hf_ref, hb_ref, mo_ref, na_ref, nw_ref, w_ref, b_ref, l1w_ref, l1b_ref,
                      kv_ref, wq_ref, wo_ref, bo_ref, l2w_ref, l2b_ref, o_ref,
                      *, ml_heads, xa_heads, alpha):
    d = h_ref.shape[1]
    width = hf_ref.shape[1]
    mdh = width // ml_heads
    dh = d // xa_heads
    scale = dh ** -0.5
    rg = h_ref.shape[0] // XA_ROW_SPLIT
    groups = [slice(r * rg, (r + 1) * rg) for r in range(XA_ROW_SPLIT)]

    def mixer_out(rows):
        hm = hf_ref[rows, :].astype(F32) + hb_ref[rows, :].astype(F32)
        parts = []
        for hd in range(ml_heads):
            x = hm[:, hd * mdh:(hd + 1) * mdh]
            mu = jnp.mean(x, axis=-1, keepdims=True)
            xc = x - mu
            var = jnp.mean(xc * xc, axis=-1, keepdims=True)
            parts.append(xc * lax.rsqrt(var + LN_EPS))
        hn = jnp.concatenate(parts, axis=1)
        ml = jax.nn.sigmoid(mo_ref[rows, :].astype(F32)) * (hn * nw_ref[...])
        y = (jnp.dot(ml.astype(BF16), w_ref[0:width, :], preferred_element_type=F32)
             + jnp.dot(na_ref[rows, :], w_ref[width:, :], preferred_element_type=F32) + b_ref[...])
        return _layer_norm(alpha * h_ref[rows, :] + y, l1w_ref[...], l1b_ref[...])

    h1 = [mixer_out(rows) for rows in groups]
    q = [jnp.dot(x.astype(BF16), wq_ref[...], preferred_element_type=F32).astype(BF16) for x in h1]
    s = [[lax.dot_general(qr[:, hd * dh:(hd + 1) * dh], kv_ref[:, hd * dh:(hd + 1) * dh], _NT,
                          preferred_element_type=F32) * scale for hd in range(xa_heads)] for qr in q]
    p = [[jnp.exp(x - jnp.max(x, axis=1, keepdims=True)) for x in sr] for sr in s]
    o = [jnp.concatenate(
        [(jnp.dot(x.astype(BF16), kv_ref[:, d + hd * dh:d + (hd + 1) * dh],
                  preferred_element_type=F32) / jnp.sum(x, axis=1, keepdims=True)).astype(BF16)
         for hd, x in enumerate(pr)], axis=1) for pr in p]
    y = [jnp.dot(orow, wo_ref[...], preferred_element_type=F32) + bo_ref[...] for orow in o]
    for rows, x, yr in zip(groups, h1, y):
        o_ref[rows, :] = _layer_norm(alpha * x + yr, l2w_ref[...], l2b_ref[...])


def _mix_xattn(h, hf, hb, mo, na, l, nw, w, b, l1w, l1b, kv, wq, wo, bo, l2w, l2b,
               batch, tm, ml_heads, xa_heads, alpha):
    m, d = h.shape
    width = hf.shape[1]
    nt = m // batch // tm
    mem_len = kv.shape[1] // batch
    row = lambda bi, i: (bi * nt + i, 0)
    layer = lambda bi, i: (l, 0, 0)
    vec = pl.BlockSpec((None, 1, d), layer)
    mat = pl.BlockSpec((None, d, d), layer)
    return pl.pallas_call(
        functools.partial(_mix_xattn_kernel, ml_heads=ml_heads, xa_heads=xa_heads, alpha=alpha),
        grid=(batch, nt),
        in_specs=[pl.BlockSpec((tm, d), row)] + [pl.BlockSpec((tm, width), row)] * 4
                 + [pl.BlockSpec((None, 1, width), layer), mat, vec, vec, vec,
                    pl.BlockSpec((None, mem_len, 2 * d), lambda bi, i: (l, bi, 0)),
                    mat, mat, vec, vec, vec],
        out_specs=pl.BlockSpec((tm, d), row),
        out_shape=jax.ShapeDtypeStruct((m, d), F32),
        compiler_params=_params("parallel", "parallel"),
        name="mixer_out_xattn_ln",
    )(h, hf, hb, mo, na, nw, w, b, l1w, l1b, kv, wq, wo, bo, l2w, l2b)


FFN_HALO = 16
FFN_SUB = 256


def _ffn_kernel(h_ref, hp_ref, hn_ref, w_ref, b_ref, dw_ref, cb_ref, wd_ref, bd_ref, lw_ref, lb_ref,
                o_ref, hb_ref, x_ref, y_ref, *, alpha):
    i = pl.program_id(1)
    tm = h_ref.shape[0]
    hal = FFN_HALO
    sb = FFN_SUB
    dff = y_ref.shape[1]
    nsub = dff // sb
    first = i == 0
    last = i == pl.num_programs(1) - 1

    hb_ref[0:hal, :] = hp_ref[...].astype(BF16)
    hb_ref[hal:hal + tm, :] = h_ref[...].astype(BF16)
    hb_ref[hal + tm:, :] = hn_ref[...].astype(BF16)

    def sides(c):
        return ((slice(0, sb), slice(c * sb, (c + 1) * sb)),
                (slice(sb, 2 * sb), slice(dff + c * sb, dff + (c + 1) * sb)))

    def project(c):
        for lanes, cols in sides(c):
            x = jnp.dot(hb_ref[...], w_ref[:, cols], preferred_element_type=F32)
            nb = -b_ref[:, cols]
            x_ref[c % 2, 0:hal, lanes] = jnp.where(first, nb, x[0:hal])
            x_ref[c % 2, hal:hal + tm, lanes] = x[hal:hal + tm]
            x_ref[c % 2, hal + tm:, lanes] = jnp.where(last, nb, x[hal + tm:])

    def gate(c):
        xs = x_ref.at[c % 2]
        conv = []
        for lanes, cols in sides(c):
            taps = [dw_ref[k:k + 1, cols] for k in range(3)]
            bias = b_ref[:, cols] * (taps[0] + taps[1] + taps[2]) + cb_ref[:, cols]
            conv.append(taps[0] * xs[hal - 1:hal - 1 + tm, lanes] + taps[1] * xs[hal:hal + tm, lanes]
                        + taps[2] * xs[hal + 1:hal + 1 + tm, lanes] + bias)
        g, u = conv
        y = (0.5 * g * (1.0 + lax.erf(g * (2.0 ** -0.5)))) * u
        y_ref[:, c * sb:(c + 1) * sb] = y.astype(BF16)

    project(0)
    for c in range(nsub):
        if c + 1 < nsub:
            project(c + 1)
        gate(c)
    down = jnp.dot(y_ref[...], wd_ref[...], preferred_element_type=F32)
    o_ref[...] = _layer_norm(alpha * h_ref[...] + down + bd_ref[...], lw_ref[...], lb_ref[...])


def _ffn(h, l, w_up, b_up, w_dw, b_dw, w_down, b_down, lw, lb, batch, tm, alpha):
    m, d = h.shape
    dff = w_down.shape[1]
    t = m // batch
    nt = t // tm
    hal = FFN_HALO
    row = lambda b, i: (b * nt + i, 0)
    prev = lambda b, i: (jnp.maximum((b * t + i * tm) // hal - 1, 0), 0)
    nxt = lambda b, i: (jnp.minimum((b * t + (i + 1) * tm) // hal, m // hal - 1), 0)
    layer = lambda b, i: (l, 0, 0)
    resident = dict(pipeline_mode=pl.Buffered(1))
    return pl.pallas_call(
        functools.partial(_ffn_kernel, alpha=alpha),
        grid=(batch, nt),
        in_specs=[pl.BlockSpec((tm, d), row), pl.BlockSpec((hal, d), prev), pl.BlockSpec((hal, d), nxt),
                  pl.BlockSpec((None, d, 2 * dff), layer, **resident),
                  pl.BlockSpec((None, 1, 2 * dff), layer),
                  pl.BlockSpec((None, 3, 2 * dff), layer),
                  pl.BlockSpec((None, 1, 2 * dff), layer),
                  pl.BlockSpec((None, dff, d), layer, **resident)]
                 + [pl.BlockSpec((None, 1, d), layer)] * 3,
        out_specs=pl.BlockSpec((tm, d), row),
        out_shape=jax.ShapeDtypeStruct((m, d), F32),
        scratch_shapes=[pltpu.VMEM((tm + 2 * hal, d), BF16),
                        pltpu.VMEM((2, tm + 2 * hal, 2 * FFN_SUB), F32),
                        pltpu.VMEM((tm, dff), BF16)],
        compiler_params=_params("parallel", "arbitrary"),
        name="conv_ffn_ln",
    )(h, h, h, w_up, b_up, w_dw, b_dw, w_down, b_down, lw, lb)


def _pick(n, prefs):
    for p in prefs:
        if n % p == 0:
            return p
    return n


def kernel(x, mem, ln_in_w, ln_in_b, w_in, b_in, ml_norm_w, na_rpb, w_mix_out, b_mix_out,
           ln1_w, ln1_b, w_xq, w_xkv, w_xo, b_xo, ln2_w, ln2_b,
           w_up, b_up, w_dw, b_dw, w_down, b_down, ln3_w, ln3_b):
    batch, t, d = x.shape
    depth = w_in.shape[0]
    m = batch * t
    ml_width = d // 2
    na_width = d - ml_width
    dff = w_down.shape[1]
    alpha = (2.0 * depth) ** 0.25
    assert ml_width == na_width and t % (GRID_W * NA_ROWS) == 0 and t % ML_CHUNK == 0

    tm = _pick(t, (512, 256, 128))
    tm_ffn = _pick(t, (1024, 512, 256, 128))

    row2 = lambda a: a.reshape(1, -1)
    row3 = lambda a: a[:, None, :]
    col3 = lambda a: a[:, :, None]
    bf = lambda a: a.astype(BF16)
    k0, k1 = ml_width, 2 * ml_width
    g0 = 4 * ml_width
    g1 = g0 + 4 * ML_HEADS
    w_main = bf(jnp.concatenate([w_in[:, :, :k0], w_in[:, :, k1:g0], w_in[:, :, g1:]], axis=2))
    b_main = row3(jnp.concatenate([b_in[:, :k0], b_in[:, k1:g0], b_in[:, g1:]], axis=1))
    wk_t = bf(jnp.swapaxes(w_in[:, :, k0:k1], 1, 2))
    bk = col3(b_in[:, k0:k1])
    wg_t = bf(jnp.swapaxes(w_in[:, :, g0:g1], 1, 2))
    bg = col3(b_in[:, g0:g1])
    bias_tab = jax.vmap(_na_bias_table)(na_rpb)
    w_out, w_q, w_o = bf(w_mix_out), bf(w_xq), bf(w_xo)
    w_upb, w_dn = bf(w_up), bf(w_down)

    h = _input_ln(x.reshape(m, d), row2(ln_in_w), row2(ln_in_b), tm)
    kv_all = _kvproj(mem.reshape(-1, d), bf(w_xkv))

    for l in range(depth):
        mq, mv, mo, nq, nk, nv, mk_t, gates = _inproj(h, l, w_main, b_main, wk_t, bk, wg_t, bg,
                                                     tm, ml_width)
        hf, hb = _mlstm(mq, mk_t, mv, gates, batch, ML_HEADS)
        na = _na(nq, nk, nv, l, bias_tab, batch, NA_HEADS)
        h = _mix_xattn(h, hf, hb, mo, na, l, row3(ml_norm_w), w_out, row3(b_mix_out),
                       row3(ln1_w), row3(ln1_b), kv_all, w_q, w_o, row3(b_xo),
                       row3(ln2_w), row3(ln2_b), batch, tm, ML_HEADS, XA_HEADS, alpha)
        h = _ffn(h, l, w_upb, row3(b_up), w_dw, row3(b_dw), w_dn, row3(b_down), row3(ln3_w), row3(ln3_b),
                 batch, tm_ffn, alpha)
    return h.reshape(batch, t, d)
```

```python
import functools
import math

import jax
import jax.numpy as jnp
from jax import lax
from jax.experimental import pallas as pl
from jax.experimental.pallas import tpu as pltpu

F32 = jnp.float32
BF16 = jnp.bfloat16

LN_EPS = 1e-5
GRID_W = 64
ML_HEADS = 4
ML_CHUNK = 128
NA_HEADS = 8
NA_ROWS = 8
NA_COLS = 16
XA_HEADS = 4

VMEM_LIMIT_BYTES = 56 * 1024 * 1024

_NT = (((1,), (1,)), ((), ()))
_TN = (((0,), (0,)), ((), ()))


def _params(*sem):
    return pltpu.CompilerParams(dimension_semantics=sem, vmem_limit_bytes=VMEM_LIMIT_BYTES)


def _layer_norm(x, w, b):
    mu = jnp.mean(x, axis=-1, keepdims=True)
    xc = x - mu
    var = jnp.mean(xc * xc, axis=-1, keepdims=True)
    return xc * lax.rsqrt(var + LN_EPS) * w + b


def _ln_kernel(x_ref, w_ref, b_ref, o_ref):
    o_ref[...] = _layer_norm(x_ref[...], w_ref[...], b_ref[...])


def _input_ln(x, w, b, tm):
    m, d = x.shape
    return pl.pallas_call(
        _ln_kernel,
        grid=(m // tm,),
        in_specs=[pl.BlockSpec((tm, d), lambda i: (i, 0)),
                  pl.BlockSpec((1, d), lambda i: (0, 0)),
                  pl.BlockSpec((1, d), lambda i: (0, 0))],
        out_specs=pl.BlockSpec((tm, d), lambda i: (i, 0)),
        out_shape=jax.ShapeDtypeStruct((m, d), F32),
        compiler_params=_params("parallel"),
        name="input_ln",
    )(x, w, b)


def _scan_chunks(x, reverse, op, fill):
    n = x.shape[-1]
    pos = lax.broadcasted_iota(jnp.int32, x.shape, 1) % ML_CHUNK
    k = 1
    while k < ML_CHUNK:
        if reverse:
            x = op(x, jnp.where(pos < ML_CHUNK - k, pltpu.roll(x, n - k, axis=1), fill))
        else:
            x = op(x, jnp.where(pos >= k, pltpu.roll(x, k, axis=1), fill))
        k *= 2
    return x


def _inproj_kernel(h_ref, w_ref, b_ref, wk_ref, bk_ref, wg_ref, bg_ref,
                   mq_ref, mv_ref, mo_ref, nq_ref, nk_ref, nv_ref, kt_ref, g_ref, *, width):
    hb = h_ref[...].astype(BF16)
    g = lax.dot_general(wg_ref[...], hb, _NT, preferred_element_type=F32) + bg_ref[...]
    heads = g.shape[0] // 4
    rows = []
    for d in range(2):
        ig = g[2 * d * heads:(2 * d + 1) * heads, :]
        lf = jax.nn.log_sigmoid(g[(2 * d + 1) * heads:(2 * d + 2) * heads, :])
        a = _scan_chunks(lf, d == 1, jnp.add, 0.0)
        b = ig - a
        rows += [a, b, _scan_chunks(b, d == 1, jnp.maximum, -jnp.inf)]
    g_ref[...] = jnp.concatenate(rows, axis=0)
    outs = (mq_ref, mv_ref, mo_ref, nq_ref, nk_ref, nv_ref)
    for j, o_ref in enumerate(outs):
        cols = slice(j * width, (j + 1) * width)
        acc = jnp.dot(hb, w_ref[:, cols], preferred_element_type=F32) + b_ref[:, cols]
        o_ref[...] = acc.astype(o_ref.dtype)
    kt = lax.dot_general(wk_ref[...], hb, _NT, preferred_element_type=F32) + bk_ref[...]
    kt_ref[...] = kt.astype(kt_ref.dtype)


def _inproj(h, l, w_main, b_main, wk_t, bk, wg_t, bg, tm, width):
    m, d = h.shape
    ncol = w_main.shape[2]
    ng = wg_t.shape[1]
    ns = 6 * (ng // 4)
    row = lambda i: (i, 0)
    col = lambda i: (0, i)
    layer = lambda i: (l, 0, 0)
    seg = lambda dt: jax.ShapeDtypeStruct((m, width), dt)
    seg_spec = pl.BlockSpec((tm, width), row)
    return pl.pallas_call(
        functools.partial(_inproj_kernel, width=width),
        grid=(m // tm,),
        in_specs=[pl.BlockSpec((tm, d), row),
                  pl.BlockSpec((None, d, ncol), layer),
                  pl.BlockSpec((None, 1, ncol), layer),
                  pl.BlockSpec((None, width, d), layer),
                  pl.BlockSpec((None, width, 1), layer),
                  pl.BlockSpec((None, ng, d), layer),
                  pl.BlockSpec((None, ng, 1), layer)],
        out_specs=[seg_spec] * 6 + [pl.BlockSpec((width, tm), col), pl.BlockSpec((ns, tm), col)],
        out_shape=[seg(BF16)] * 6 + [
                   jax.ShapeDtypeStruct((width, m), BF16), jax.ShapeDtypeStruct((ns, m), F32)],
        compiler_params=_params("parallel"),
        name="mixer_inproj",
    )(h, w_main, b_main, wk_t, bk, wg_t, bg)


ML_BATCH_PER_STEP = 2


def _mlstm_kernel(*refs, heads, dh):
    L = ML_CHUNK
    scale = dh ** -0.5
    nb = ML_BATCH_PER_STEP
    per_dir = 2 + 2 * nb
    ins, outs, (c_ref, m_ref) = refs[:2 * per_dir], refs[2 * per_dir:2 * per_dir + 2], refs[-2:]

    @pl.when(pl.program_id(1) == 0)
    def _():
        c_ref[...] = jnp.zeros_like(c_ref)
        m_ref[...] = jnp.zeros_like(m_ref)

    t_idx = lax.broadcasted_iota(jnp.int32, (L, L), 0)
    s_idx = lax.broadcasted_iota(jnp.int32, (L, L), 1)

    def col_replicated(x_row):
        return jnp.broadcast_to(x_row, (L, L)).T

    ones = jnp.ones((L, dh), BF16)
    probs = []
    for jd in range(2 * nb):
        d, j = jd // nb, jd % nb
        q_ref, v_ref = ins[d * per_dir].at[j], ins[d * per_dir + 1].at[j]
        kt_ref, g_ref = ins[d * per_dir + 2 + j], ins[d * per_dir + 2 + nb + j]
        o_ref = outs[d].at[j]
        reverse = d == 1
        r0 = 3 * d * heads
        a = g_ref[r0:r0 + heads, :]
        b = g_ref[r0 + heads:r0 + 2 * heads, :]
        b_run = g_ref[r0 + 2 * heads:r0 + 3 * heads, :]
        m_prev = m_ref[jd]
        g_run = jnp.maximum(m_prev, b_run)
        a_end = a[:, 0:1] if reverse else a[:, L - 1:L]
        up_log = a_end + b
        m_new = jnp.maximum(a_end + m_prev, jnp.max(up_log, axis=1, keepdims=True))
        wk = jnp.exp(up_log - m_new) * scale
        decay = jnp.exp(a_end + m_prev - m_new)
        m_ref[jd] = m_new
        causal = (s_idx >= t_idx) if reverse else (s_idx <= t_idx)
        for hd in range(heads):
            row = slice(hd, hd + 1)
            probs.append(dict(
                st=jd * heads + hd, cols=slice(hd * dh, (hd + 1) * dh), causal=causal,
                q_ref=q_ref, kt_ref=kt_ref, v_ref=v_ref, o_ref=o_ref,
                a=a[row], b=b[row], g=g_run[row], m_prev=m_prev[row], wk=wk[row], decay=decay[row]))

    for p in probs:
        p["a_cb"] = col_replicated(p["a"])
        p["g_cb"] = col_replicated(p["g"])
    for p in probs:
        p["q"] = p["q_ref"][:, p["cols"]]
        p["s"] = jnp.dot(p["q"], p["kt_ref"][p["cols"], :], preferred_element_type=F32)
    for p in probs:
        w = jnp.where(p["causal"], jnp.exp(p["b"] - p["g_cb"]), 0.0)
        p["w_qk"] = (w * (p["s"] * scale)).astype(BF16)
        p["inter"] = jnp.exp(p["m_prev"] - p["g_cb"])
        p["floor"] = jnp.exp(-(p["a_cb"] + p["g_cb"]))
    for p in probs:
        p["v_aug"] = jnp.concatenate([p["v_ref"][:, p["cols"]], ones], axis=1)
        p["c_aug"] = c_ref[p["st"]]
        q_c = jnp.dot(p["q"], p["c_aug"].astype(BF16), preferred_element_type=F32)
        w_v = jnp.dot(p["w_qk"], p["v_aug"], preferred_element_type=F32)
        num = w_v[:, :dh] + p["inter"] * q_c[:, :dh]
        den = w_v[:, dh:] + p["inter"] * q_c[:, dh:]
        h_out = num / jnp.maximum(jnp.abs(den), p["floor"])
        p["o_ref"][:, p["cols"]] = h_out.astype(p["o_ref"].dtype)
    for p in probs:
        kw_t = (p["kt_ref"][p["cols"], :].astype(F32) * p["wk"]).astype(BF16)
        decay2 = jnp.concatenate([p["decay"], p["decay"]], axis=1)
        c_ref[p["st"]] = decay2 * p["c_aug"] + jnp.dot(kw_t, p["v_aug"], preferred_element_type=F32)


def _mlstm(mq, mk_t, mv, gates, batch, heads):
    m, width = mq.shape
    t = m // batch
    dh = width // heads
    nc = t // ML_CHUNK
    ng = gates.shape[0]
    nb = ML_BATCH_PER_STEP
    grouped = lambda a: a.reshape(batch // nb, nb, t, width)
    in_specs, operands, out_specs = [], [], []
    for reverse in (False, True):
        chunk = (lambda c: nc - 1 - c) if reverse else (lambda c: c)
        rows = pl.BlockSpec((None, nb, ML_CHUNK, width), lambda bb, c, chunk=chunk: (bb, 0, chunk(c), 0))
        cols = [lambda bb, c, chunk=chunk, j=j: (0, (bb * nb + j) * nc + chunk(c)) for j in range(nb)]
        in_specs += ([rows, rows] + [pl.BlockSpec((width, ML_CHUNK), cm) for cm in cols]
                     + [pl.BlockSpec((ng, ML_CHUNK), cm) for cm in cols])
        operands += [grouped(mq), grouped(mv)] + [mk_t] * nb + [gates] * nb
        out_specs.append(rows)
    hf, hb = pl.pallas_call(
        functools.partial(_mlstm_kernel, heads=heads, dh=dh),
        grid=(batch // nb, nc),
        in_specs=in_specs,
        out_specs=out_specs,
        out_shape=[jax.ShapeDtypeStruct((batch // nb, nb, t, width), BF16)] * 2,
        scratch_shapes=[pltpu.VMEM((2 * nb * heads, dh, 2 * dh), F32),
                        pltpu.VMEM((2 * nb, heads, ML_CHUNK), F32)],
        compiler_params=_params("parallel", "arbitrary"),
        name="mlstm_scan",
    )(*operands)
    return hf.reshape(m, width), hb.reshape(m, width)


def _na_bias_table(rpb):
    j = jnp.arange(GRID_W)
    dc = jnp.clip(j[None, :] - j[:, None] + NA_COLS - 1, 0, 2 * NA_COLS - 2)
    cs = jnp.clip(j - NA_COLS // 2, 0, GRID_W - NA_COLS)
    ok = (j[None, :] >= cs[:, None]) & (j[None, :] < cs[:, None] + NA_COLS)
    onehot = (dc[..., None] == jnp.arange(2 * NA_COLS - 1)).astype(F32)
    cols = jnp.einsum('hdc,qkc->hdqk', rpb.astype(F32), onehot, precision=lax.Precision.HIGHEST)
    cols = jnp.where(ok[None, None], cols, -jnp.inf)
    pair = jnp.stack([cols[:, :-1], cols[:, 1:]], axis=-2)
    return pair.reshape(pair.shape[:-2] + (2 * GRID_W,))


def _na_window_start(r, rows):
    return jnp.clip(r - NA_ROWS // 2, 0, rows - NA_ROWS)


NA_GROUP = 4
NA_KEY_TILES = 3


def _na_kernel(q_ref, k_ref, v_ref, bias_ref, o_ref, *, rows, heads, dh):
    g = pl.program_id(1)
    ktile = NA_GROUP * GRID_W
    nkeys = NA_KEY_TILES * ktile
    scale = dh ** -0.5
    assert math.frexp(scale)[0] == 0.5, "the score scale must be a power of two to fold it into q"
    pw = 2 * dh
    npairs = nkeys // (2 * GRID_W)
    jt = jnp.clip(g - 1, 0, rows // NA_GROUP - NA_KEY_TILES)
    krow0 = jt * NA_GROUP
    tok0 = pl.multiple_of(krow0 * GRID_W, ktile)
    lane_lo = lax.broadcasted_iota(jnp.int32, (GRID_W, pw), 1) < dh
    ones = jnp.ones((nkeys, pw), BF16)
    neg = jnp.full((GRID_W, pw), -jnp.inf, F32)
    pair_row = jnp.where(lane_lo, 0, 1)

    def bias_rows(r, head):
        start = _na_window_start(r, rows)
        blocks = []
        for p in range(npairs):
            kr = krow0 + 2 * p
            dr = jnp.clip(kr - r + NA_ROWS - 1, 0, 2 * NA_ROWS - 3)
            krow = kr + pair_row
            ok = jnp.logical_and(krow >= start, krow < start + NA_ROWS)
            blocks.append(jnp.where(ok, bias_ref[head, dr], neg))
        return jnp.concatenate(blocks, axis=1)

    s = []
    for hp in range(heads // 2):
        pcols = slice(hp * pw, (hp + 1) * pw)
        qs, bs = [], []
        for i in range(NA_GROUP):
            q = q_ref[i * GRID_W:(i + 1) * GRID_W, pcols] * scale
            for half in range(2):
                qs.append(jnp.where(lane_lo if half == 0 else jnp.logical_not(lane_lo), q,
                                    jnp.zeros_like(q)))
                bs.append(bias_rows(g * NA_GROUP + i, 2 * hp + half))
        k = k_ref[pl.ds(tok0, nkeys), pcols]
        s.append(lax.dot_general(jnp.concatenate(qs, axis=0), k, _NT,
                                 preferred_element_type=F32) + jnp.concatenate(bs, axis=0))
    p = [jnp.exp(x - jnp.max(x, axis=1, keepdims=True)).astype(BF16) for x in s]
    outs = []
    for hp, x in enumerate(p):
        v_aug = jnp.concatenate([v_ref[pl.ds(tok0, nkeys), hp * pw:(hp + 1) * pw], ones], axis=1)
        res = jnp.dot(x, v_aug, preferred_element_type=F32)
        per_row = []
        for i in range(NA_GROUP):
            lo = res[(2 * i) * GRID_W:(2 * i + 1) * GRID_W]
            hi = res[(2 * i + 1) * GRID_W:(2 * i + 2) * GRID_W]
            both = jnp.where(jnp.concatenate([lane_lo, lane_lo], axis=1), lo, hi)
            per_row.append(both[:, :pw] / both[:, pw:])
        outs.append(jnp.concatenate(per_row, axis=0))
    o_ref[...] = jnp.concatenate(outs, axis=1).astype(o_ref.dtype)


def _na(nq, nk, nv, l, bias_tab, batch, heads):
    m, width = nq.shape
    t = m // batch
    rows = t // GRID_W
    dh = width // heads
    steps = rows // NA_GROUP
    qblk = pl.BlockSpec((NA_GROUP * GRID_W, width), lambda b, r: (b * steps + r, 0))
    return pl.pallas_call(
        functools.partial(_na_kernel, rows=rows, heads=heads, dh=dh),
        grid=(batch, steps),
        in_specs=[qblk,
                  pl.BlockSpec((t, width), lambda b, r: (b, 0)),
                  pl.BlockSpec((t, width), lambda b, r: (b, 0)),
                  pl.BlockSpec((None, heads, 2 * NA_ROWS - 2, GRID_W, 2 * GRID_W),
                               lambda b, r: (l, 0, 0, 0, 0))],
        out_specs=qblk,
        out_shape=jax.ShapeDtypeStruct((m, width), BF16),
        compiler_params=_params("parallel", "arbitrary"),
        name="neighborhood_attn",
    )(nq, nk, nv, bias_tab)


def _kvproj_kernel(mem_ref, w_ref, o_ref):
    o_ref[0] = jnp.dot(mem_ref[...].astype(BF16), w_ref[0],
                       preferred_element_type=F32).astype(o_ref.dtype)


def _kvproj(mem, w_kv):
    mm, d = mem.shape
    depth, _, n = w_kv.shape
    return pl.pallas_call(
        _kvproj_kernel,
        grid=(depth,),
        in_specs=[pl.BlockSpec((mm, d), lambda l: (0, 0)),
                  pl.BlockSpec((1, d, n), lambda l: (l, 0, 0))],
        out_specs=pl.BlockSpec((1, mm, n), lambda l: (l, 0, 0)),
        out_shape=jax.ShapeDtypeStruct((depth, mm, n), BF16),
        compiler_params=_params("parallel"),
        name="memory_kv_proj",
    )(mem, w_kv)


XA_ROW_SPLIT = 2


def _mix_xattn_kernel(h_ref, hf_ref, hb_ref, mo_ref, na_ref, nw_ref, w_ref, b_ref, l1w_ref, l1b_ref,
                      kv_ref, wq_ref, wo_ref, bo_ref, l2w_ref, l2b_ref, o_ref,
                      *, ml_heads, xa_heads, alpha):
    d = h_ref.shape[1]
    width = hf_ref.shape[1]
    mdh = width // ml_heads
    dh = d // xa_heads
    scale = dh ** -0.5
    rg = h_ref.shape[0] // XA_ROW_SPLIT
    groups = [slice(r * rg, (r + 1) * rg) for r in range(XA_ROW_SPLIT)]

    def mixer_out(rows):
        hm = hf_ref[rows, :].astype(F32) + hb_ref[rows, :].astype(F32)
        parts = []
        for hd in range(ml_heads):
            x = hm[:, hd * mdh:(hd + 1) * mdh]
            mu = jnp.mean(x, axis=-1, keepdims=True)
            xc = x - mu
            var = jnp.mean(xc * xc, axis=-1, keepdims=True)
            parts.append(xc * lax.rsqrt(var + LN_EPS))
        hn = jnp.concatenate(parts, axis=1)
        ml = jax.nn.sigmoid(mo_ref[rows, :].astype(F32)) * (hn * nw_ref[...])
        y = (jnp.dot(ml.astype(BF16), w_ref[0:width, :], preferred_element_type=F32)
             + jnp.dot(na_ref[rows, :], w_ref[width:, :], preferred_element_type=F32) + b_ref[...])
        return _layer_norm(alpha * h_ref[rows, :] + y, l1w_ref[...], l1b_ref[...])

    h1 = [mixer_out(rows) for rows in groups]
    q = [jnp.dot(x.astype(BF16), wq_ref[...], preferred_element_type=F32).astype(BF16) for x in h1]
    s = [[lax.dot_general(qr[:, hd * dh:(hd + 1) * dh], kv_ref[:, hd * dh:(hd + 1) * dh], _NT,
                          preferred_element_type=F32) * scale for hd in range(xa_heads)] for qr in q]
    p = [[jnp.exp(x - jnp.max(x, axis=1, keepdims=True)) for x in sr] for sr in s]
    o = [jnp.concatenate(
        [(jnp.dot(x.astype(BF16), kv_ref[:, d + hd * dh:d + (hd + 1) * dh],
                  preferred_element_type=F32) / jnp.sum(x, axis=1, keepdims=True)).astype(BF16)
         for hd, x in enumerate(pr)], axis=1) for pr in p]
    y = [jnp.dot(orow, wo_ref[...], preferred_element_type=F32) + bo_ref[...] for orow in o]
    for rows, x, yr in zip(groups, h1, y):
        o_ref[rows, :] = _layer_norm(alpha * x + yr, l2w_ref[...], l2b_ref[...])


def _mix_xattn(h, hf, hb, mo, na, l, nw, w, b, l1w, l1b, kv, wq, wo, bo, l2w, l2b,
               batch, tm, ml_heads, xa_heads, alpha):
    m, d = h.shape
    width = hf.shape[1]
    nt = m // batch // tm
    mem_len = kv.shape[1] // batch
    row = lambda bi, i: (bi * nt + i, 0)
    layer = lambda bi, i: (l, 0, 0)
    vec = pl.BlockSpec((None, 1, d), layer)
    mat = pl.BlockSpec((None, d, d), layer)
    return pl.pallas_call(
        functools.partial(_mix_xattn_kernel, ml_heads=ml_heads, xa_heads=xa_heads, alpha=alpha),
        grid=(batch, nt),
        in_specs=[pl.BlockSpec((tm, d), row)] + [pl.BlockSpec((tm, width), row)] * 4
                 + [pl.BlockSpec((None, 1, width), layer), mat, vec, vec, vec,
                    pl.BlockSpec((None, mem_len, 2 * d), lambda bi, i: (l, bi, 0)),
                    mat, mat, vec, vec, vec],
        out_specs=pl.BlockSpec((tm, d), row),
        out_shape=jax.ShapeDtypeStruct((m, d), F32),
        compiler_params=_params("parallel", "parallel"),
        name="mixer_out_xattn_ln",
    )(h, hf, hb, mo, na, nw, w, b, l1w, l1b, kv, wq, wo, bo, l2w, l2b)


FFN_HALO = 16
FFN_SUB = 256


def _ffn_kernel(h_ref, hp_ref, hn_ref, w_ref, b_ref, dw_ref, cb_ref, wd_ref, bd_ref, lw_ref, lb_ref,
                o_ref, hb_ref, x_ref, y_ref, *, alpha):
    i = pl.program_id(1)
    tm = h_ref.shape[0]
    hal = FFN_HALO
    sb = FFN_SUB
    dff = y_ref.shape[1]
    nsub = dff // sb
    first = i == 0
    last = i == pl.num_programs(1) - 1

    hb_ref[0:hal, :] = hp_ref[...].astype(BF16)
    hb_ref[hal:hal + tm, :] = h_ref[...].astype(BF16)
    hb_ref[hal + tm:, :] = hn_ref[...].astype(BF16)

    def sides(c):
        return ((slice(0, sb), slice(c * sb, (c + 1) * sb)),
                (slice(sb, 2 * sb), slice(dff + c * sb, dff + (c + 1) * sb)))

    def project(c):
        for lanes, cols in sides(c):
            x = jnp.dot(hb_ref[...], w_ref[:, cols], preferred_element_type=F32)
            nb = -b_ref[:, cols]
            x_ref[c % 2, 0:hal, lanes] = jnp.where(first, nb, x[0:hal])
            x_ref[c % 2, hal:hal + tm, lanes] = x[hal:hal + tm]
            x_ref[c % 2, hal + tm:, lanes] = jnp.where(last, nb, x[hal + tm:])

    def gate(c):
        xs = x_ref.at[c % 2]
        conv = []
        for lanes, cols in sides(c):
            taps = [dw_ref[k:k + 1, cols] for k in range(3)]
            bias = b_ref[:, cols] * (taps[0] + taps[1] + taps[2]) + cb_ref[:, cols]
            conv.append(taps[0] * xs[hal - 1:hal - 1 + tm, lanes] + taps[1] * xs[hal:hal + tm, lanes]
                        + taps[2] * xs[hal + 1:hal + 1 + tm, lanes] + bias)
        g, u = conv
        y = (0.5 * g * (1.0 + lax.erf(g * (2.0 ** -0.5)))) * u
        y_ref[:, c * sb:(c + 1) * sb] = y.astype(BF16)

    project(0)
    for c in range(nsub):
        if c + 1 < nsub:
            project(c + 1)
        gate(c)
    down = jnp.dot(y_ref[...], wd_ref[...], preferred_element_type=F32)
    o_ref[...] = _layer_norm(alpha * h_ref[...] + down + bd_ref[...], lw_ref[...], lb_ref[...])


def _ffn(h, l, w_up, b_up, w_dw, b_dw, w_down, b_down, lw, lb, batch, tm, alpha):
    m, d = h.shape
    dff = w_down.shape[1]
    t = m // batch
    nt = t // tm
    hal = FFN_HALO
    row = lambda b, i: (b * nt + i, 0)
    prev = lambda b, i: (jnp.maximum((b * t + i * tm) // hal - 1, 0), 0)
    nxt = lambda b, i: (jnp.minimum((b * t + (i + 1) * tm) // hal, m // hal - 1), 0)
    layer = lambda b, i: (l, 0, 0)
    resident = dict(pipeline_mode=pl.Buffered(1))
    return pl.pallas_call(
        functools.partial(_ffn_kernel, alpha=alpha),
        grid=(batch, nt),
        in_specs=[pl.BlockSpec((tm, d), row), pl.BlockSpec((hal, d), prev), pl.BlockSpec((hal, d), nxt),
                  pl.BlockSpec((None, d, 2 * dff), layer, **resident),
                  pl.BlockSpec((None, 1, 2 * dff), layer),
                  pl.BlockSpec((None, 3, 2 * dff), layer),
                  pl.BlockSpec((None, 1, 2 * dff), layer),
                  pl.BlockSpec((None, dff, d), layer, **resident)]
                 + [pl.BlockSpec((None, 1, d), layer)] * 3,
        out_specs=pl.BlockSpec((tm, d), row),
        out_shape=jax.ShapeDtypeStruct((m, d), F32),
        scratch_shapes=[pltpu.VMEM((tm + 2 * hal, d), BF16),
                        pltpu.VMEM((2, tm + 2 * hal, 2 * FFN_SUB), F32),
                        pltpu.VMEM((tm, dff), BF16)],
        compiler_params=_params("parallel", "arbitrary"),
        name="conv_ffn_ln",
    )(h, h, h, w_up, b_up, w_dw, b_dw, w_down, b_down, lw, lb)


def _pick(n, prefs):
    for p in prefs:
        if n % p == 0:
            return p
    return n


def kernel(x, mem, ln_in_w, ln_in_b, w_in, b_in, ml_norm_w, na_rpb, w_mix_out, b_mix_out,
           ln1_w, ln1_b, w_xq, w_xkv, w_xo, b_xo, ln2_w, ln2_b,
           w_up, b_up, w_dw, b_dw, w_down, b_down, ln3_w, ln3_b):
    batch, t, d = x.shape
    depth = w_in.shape[0]
    m = batch * t
    ml_width = d // 2
    na_width = d - ml_width
    dff = w_down.shape[1]
    alpha = (2.0 * depth) ** 0.25
    assert ml_width == na_width and t % (GRID_W * NA_ROWS) == 0 and t % ML_CHUNK == 0

    tm = _pick(t, (512, 256, 128))
    tm_ffn = _pick(t, (1024, 512, 256, 128))

    row2 = lambda a: a.reshape(1, -1)
    row3 = lambda a: a[:, None, :]
    col3 = lambda a: a[:, :, None]
    bf = lambda a: a.astype(BF16)
    k0, k1 = ml_width, 2 * ml_width
    g0 = 4 * ml_width
    g1 = g0 + 4 * ML_HEADS
    w_main = bf(jnp.concatenate([w_in[:, :, :k0], w_in[:, :, k1:g0], w_in[:, :, g1:]], axis=2))
    b_main = row3(jnp.concatenate([b_in[:, :k0], b_in[:, k1:g0], b_in[:, g1:]], axis=1))
    wk_t = jnp.swapaxes(bf(w_in[:, :, k0:k1]), 1, 2)
    bk = col3(b_in[:, k0:k1])
    wg_t = jnp.swapaxes(bf(w_in[:, :, g0:g1]), 1, 2)
    bg = col3(b_in[:, g0:g1])
    bias_tab = jax.vmap(_na_bias_table)(na_rpb)
    w_out, w_q, w_o = bf(w_mix_out), bf(w_xq), bf(w_xo)
    w_upb, w_dn = bf(w_up), bf(w_down)

    h = _input_ln(x.reshape(m, d), row2(ln_in_w), row2(ln_in_b), tm)
    kv_all = _kvproj(mem.reshape(-1, d), bf(w_xkv))

    for l in range(depth):
        mq, mv, mo, nq, nk, nv, mk_t, gates = _inproj(h, l, w_main, b_main, wk_t, bk, wg_t, bg,
                                                     tm, ml_width)
        hf, hb = _mlstm(mq, mk_t, mv, gates, batch, ML_HEADS)
        na = _na(nq, nk, nv, l, bias_tab, batch, NA_HEADS)
        h = _mix_xattn(h, hf, hb, mo, na, l, row3(ml_norm_w), w_out, row3(b_mix_out),
                       row3(ln1_w), row3(ln1_b), kv_all, w_q, w_o, row3(b_xo),
                       row3(ln2_w), row3(ln2_b), batch, tm, ML_HEADS, XA_HEADS, alpha)
        h = _ffn(h, l, w_upb, row3(b_up), w_dw, row3(b_dw), w_dn, row3(b_down), row3(ln3_w), row3(ln3_b),
                 batch, tm_ffn, alpha)
    return h.reshape(batch, t, d)
```

```python
import functools
import math

import jax
import jax.numpy as jnp
from jax import lax
from jax.experimental import pallas as pl
from jax.experimental.pallas import tpu as pltpu

F32 = jnp.float32
BF16 = jnp.bfloat16

LN_EPS = 1e-5
GRID_W = 64
ML_HEADS = 4
ML_CHUNK = 128
NA_HEADS = 8
NA_ROWS = 8
NA_COLS = 16
XA_HEADS = 4

VMEM_LIMIT_BYTES = 56 * 1024 * 1024

_NT = (((1,), (1,)), ((), ()))
_TN = (((0,), (0,)), ((), ()))


def _params(*sem):
    return pltpu.CompilerParams(dimension_semantics=sem, vmem_limit_bytes=VMEM_LIMIT_BYTES)


def _layer_norm(x, w, b):
    mu = jnp.mean(x, axis=-1, keepdims=True)
    xc = x - mu
    var = jnp.mean(xc * xc, axis=-1, keepdims=True)
    return xc * lax.rsqrt(var + LN_EPS) * w + b


def _scan_chunks(x, reverse, op, fill):
    n = x.shape[-1]
    pos = lax.broadcasted_iota(jnp.int32, x.shape, 1) % ML_CHUNK
    k = 1
    while k < ML_CHUNK:
        if reverse:
            x = op(x, jnp.where(pos < ML_CHUNK - k, pltpu.roll(x, n - k, axis=1), fill))
        else:
            x = op(x, jnp.where(pos >= k, pltpu.roll(x, k, axis=1), fill))
        k *= 2
    return x


def _inproj_kernel(*refs, width, input_norm):
    if input_norm:
        h_ref, lnw_ref, lnb_ref, *refs = refs
        *refs, hn_ref = refs
    else:
        h_ref, *refs = refs
    (w_ref, b_ref, wk_ref, bk_ref, wg_ref, bg_ref,
     mq_ref, mv_ref, mo_ref, nq_ref, nk_ref, nv_ref, kt_ref, g_ref) = refs
    h = h_ref[...]
    if input_norm:
        h = _layer_norm(h, lnw_ref[...], lnb_ref[...])
        hn_ref[...] = h
    hb = h.astype(BF16)
    g = lax.dot_general(wg_ref[...], hb, _NT, preferred_element_type=F32) + bg_ref[...]
    heads = g.shape[0] // 4
    rows = []
    for d in range(2):
        ig = g[2 * d * heads:(2 * d + 1) * heads, :]
        lf = jax.nn.log_sigmoid(g[(2 * d + 1) * heads:(2 * d + 2) * heads, :])
        a = _scan_chunks(lf, d == 1, jnp.add, 0.0)
        b = ig - a
        rows += [a, b, _scan_chunks(b, d == 1, jnp.maximum, -jnp.inf)]
    g_ref[...] = jnp.concatenate(rows, axis=0)
    outs = (mq_ref, mv_ref, mo_ref, nq_ref, nk_ref, nv_ref)
    for j, o_ref in enumerate(outs):
        cols = slice(j * width, (j + 1) * width)
        acc = jnp.dot(hb, w_ref[:, cols], preferred_element_type=F32) + b_ref[:, cols]
        o_ref[...] = acc.astype(o_ref.dtype)
    kt = lax.dot_general(wk_ref[...], hb, _NT, preferred_element_type=F32) + bk_ref[...]
    kt_ref[...] = kt.astype(kt_ref.dtype)


def _inproj(h, l, w_main, b_main, wk_t, bk, wg_t, bg, tm, width, input_ln=None):
    m, d = h.shape
    ncol = w_main.shape[2]
    ng = wg_t.shape[1]
    ns = 6 * (ng // 4)
    row = lambda i: (i, 0)
    col = lambda i: (0, i)
    layer = lambda i: (l, 0, 0)
    seg = lambda dt: jax.ShapeDtypeStruct((m, width), dt)
    seg_spec = pl.BlockSpec((tm, width), row)
    fused = input_ln is not None
    ln_specs = [pl.BlockSpec((1, d), lambda i: (0, 0))] * 2 if fused else []
    ln_args = list(input_ln) if fused else []
    return pl.pallas_call(
        functools.partial(_inproj_kernel, width=width, input_norm=fused),
        grid=(m // tm,),
        in_specs=[pl.BlockSpec((tm, d), row)] + ln_specs + [
                  pl.BlockSpec((None, d, ncol), layer),
                  pl.BlockSpec((None, 1, ncol), layer),
                  pl.BlockSpec((None, width, d), layer),
                  pl.BlockSpec((None, width, 1), layer),
                  pl.BlockSpec((None, ng, d), layer),
                  pl.BlockSpec((None, ng, 1), layer)],
        out_specs=([seg_spec] * 6 + [pl.BlockSpec((width, tm), col), pl.BlockSpec((ns, tm), col)]
                   + ([pl.BlockSpec((tm, d), row)] if fused else [])),
        out_shape=([seg(BF16)] * 6 + [
                   jax.ShapeDtypeStruct((width, m), BF16), jax.ShapeDtypeStruct((ns, m), F32)]
                   + ([jax.ShapeDtypeStruct((m, d), F32)] if fused else [])),
        compiler_params=_params("parallel"),
        name="mixer_inproj",
    )(h, *ln_args, w_main, b_main, wk_t, bk, wg_t, bg)


ML_BATCH_PER_STEP = 2


def _mlstm_kernel(*refs, heads, dh):
    L = ML_CHUNK
    scale = dh ** -0.5
    nb = ML_BATCH_PER_STEP
    per_dir = 2 + 2 * nb
    ins, outs, (c_ref, m_ref) = refs[:2 * per_dir], refs[2 * per_dir:2 * per_dir + 2], refs[-2:]

    @pl.when(pl.program_id(1) == 0)
    def _():
        c_ref[...] = jnp.zeros_like(c_ref)
        m_ref[...] = jnp.zeros_like(m_ref)

    t_idx = lax.broadcasted_iota(jnp.int32, (L, L), 0)
    s_idx = lax.broadcasted_iota(jnp.int32, (L, L), 1)

    def col_replicated(x_row):
        return jnp.broadcast_to(x_row, (L, L)).T

    ones = jnp.ones((L, dh), BF16)
    probs = []
    for jd in range(2 * nb):
        d, j = jd // nb, jd % nb
        q_ref, v_ref = ins[d * per_dir].at[j], ins[d * per_dir + 1].at[j]
        kt_ref, g_ref = ins[d * per_dir + 2 + j], ins[d * per_dir + 2 + nb + j]
        o_ref = outs[d].at[j]
        reverse = d == 1
        r0 = 3 * d * heads
        a = g_ref[r0:r0 + heads, :]
        b = g_ref[r0 + heads:r0 + 2 * heads, :]
        b_run = g_ref[r0 + 2 * heads:r0 + 3 * heads, :]
        m_prev = m_ref[jd]
        g_run = jnp.maximum(m_prev, b_run)
        a_end = a[:, 0:1] if reverse else a[:, L - 1:L]
        up_log = a_end + b
        m_new = jnp.maximum(a_end + m_prev, jnp.max(up_log, axis=1, keepdims=True))
        wk = jnp.exp(up_log - m_new) * scale
        decay = jnp.exp(a_end + m_prev - m_new)
        m_ref[jd] = m_new
        causal = (s_idx >= t_idx) if reverse else (s_idx <= t_idx)
        for hd in range(heads):
            row = slice(hd, hd + 1)
            probs.append(dict(
                st=jd * heads + hd, cols=slice(hd * dh, (hd + 1) * dh), causal=causal,
                q_ref=q_ref, kt_ref=kt_ref, v_ref=v_ref, o_ref=o_ref,
                a=a[row], b=b[row], g=g_run[row], m_prev=m_prev[row], wk=wk[row], decay=decay[row]))

    for p in probs:
        p["a_cb"] = col_replicated(p["a"])
        p["g_cb"] = col_replicated(p["g"])
    for p in probs:
        p["q"] = p["q_ref"][:, p["cols"]]
        p["s"] = jnp.dot(p["q"], p["kt_ref"][p["cols"], :], preferred_element_type=F32)
    for p in probs:
        w = jnp.where(p["causal"], jnp.exp(p["b"] - p["g_cb"]), 0.0)
        p["w_qk"] = (w * (p["s"] * scale)).astype(BF16)
        p["inter"] = jnp.exp(p["m_prev"] - p["g_cb"])
        p["floor"] = jnp.exp(-(p["a_cb"] + p["g_cb"]))
    for p in probs:
        p["v_aug"] = jnp.concatenate([p["v_ref"][:, p["cols"]], ones], axis=1)
        p["c_aug"] = c_ref[p["st"]]
        q_c = jnp.dot(p["q"], p["c_aug"].astype(BF16), preferred_element_type=F32)
        w_v = jnp.dot(p["w_qk"], p["v_aug"], preferred_element_type=F32)
        num = w_v[:, :dh] + p["inter"] * q_c[:, :dh]
        den = w_v[:, dh:] + p["inter"] * q_c[:, dh:]
        h_out = num / jnp.maximum(jnp.abs(den), p["floor"])
        p["o_ref"][:, p["cols"]] = h_out.astype(p["o_ref"].dtype)
    for p in probs:
        kw_t = (p["kt_ref"][p["cols"], :].astype(F32) * p["wk"]).astype(BF16)
        decay2 = jnp.concatenate([p["decay"], p["decay"]], axis=1)
        c_ref[p["st"]] = decay2 * p["c_aug"] + jnp.dot(kw_t, p["v_aug"], preferred_element_type=F32)


def _mlstm(mq, mk_t, mv, gates, batch, heads):
    m, width = mq.shape
    t = m // batch
    dh = width // heads
    nc = t // ML_CHUNK
    ng = gates.shape[0]
    nb = ML_BATCH_PER_STEP
    assert batch % nb == 0
    grouped = lambda a: a.reshape(batch // nb, nb, t, width)
    in_specs, operands, out_specs = [], [], []
    for reverse in (False, True):
        chunk = (lambda c: nc - 1 - c) if reverse else (lambda c: c)
        rows = pl.BlockSpec((None, nb, ML_CHUNK, width), lambda bb, c, chunk=chunk: (bb, 0, chunk(c), 0))
        cols = [lambda bb, c, chunk=chunk, j=j: (0, (bb * nb + j) * nc + chunk(c)) for j in range(nb)]
        in_specs += ([rows, rows] + [pl.BlockSpec((width, ML_CHUNK), cm) for cm in cols]
                     + [pl.BlockSpec((ng, ML_CHUNK), cm) for cm in cols])
        operands += [grouped(mq), grouped(mv)] + [mk_t] * nb + [gates] * nb
        out_specs.append(rows)
    hf, hb = pl.pallas_call(
        functools.partial(_mlstm_kernel, heads=heads, dh=dh),
        grid=(batch // nb, nc),
        in_specs=in_specs,
        out_specs=out_specs,
        out_shape=[jax.ShapeDtypeStruct((batch // nb, nb, t, width), BF16)] * 2,
        scratch_shapes=[pltpu.VMEM((2 * nb * heads, dh, 2 * dh), F32),
                        pltpu.VMEM((2 * nb, heads, ML_CHUNK), F32)],
        compiler_params=_params("parallel", "arbitrary"),
        name="mlstm_scan",
    )(*operands)
    return hf.reshape(m, width), hb.reshape(m, width)


def _na_bias_table(rpb):
    q = jnp.arange(GRID_W)[:, None]
    lane = jnp.arange(2 * GRID_W)[None, :]
    k = lane % GRID_W
    dc = jnp.clip(k - q + NA_COLS - 1, 0, 2 * NA_COLS - 2)
    cs = jnp.clip(q - NA_COLS // 2, 0, GRID_W - NA_COLS)
    ok = (k >= cs) & (k < cs + NA_COLS)
    pick = ((lane // GRID_W)[..., None, None] == jnp.arange(2)[:, None]) & (
        dc[..., None, None] == jnp.arange(2 * NA_COLS - 1))
    rows2 = jnp.stack([rpb[:, :-1], rpb[:, 1:]], axis=2).astype(F32)
    tab = jnp.einsum('hdjc,qljc->hdql', rows2, pick.astype(F32), precision=lax.Precision.HIGHEST)
    return jnp.where(ok[None, None], tab, -jnp.inf)


def _na_window_start(r, rows):
    return jnp.clip(r - NA_ROWS // 2, 0, rows - NA_ROWS)


NA_GROUP = 4
NA_KEY_TILES = 3


def _na_kernel(q_ref, k_ref, v_ref, bias_ref, o_ref, *, rows, heads, dh):
    g = pl.program_id(1)
    ktile = NA_GROUP * GRID_W
    nkeys = NA_KEY_TILES * ktile
    scale = dh ** -0.5
    assert math.frexp(scale)[0] == 0.5, "the score scale must be a power of two to fold it into q"
    pw = 2 * dh
    npairs = nkeys // (2 * GRID_W)
    jt = jnp.clip(g - 1, 0, rows // NA_GROUP - NA_KEY_TILES)
    krow0 = jt * NA_GROUP
    tok0 = pl.multiple_of(krow0 * GRID_W, ktile)
    lane_lo = lax.broadcasted_iota(jnp.int32, (GRID_W, pw), 1) < dh
    ones = jnp.ones((nkeys, pw), BF16)
    neg = jnp.full((GRID_W, pw), -jnp.inf, F32)
    pair_row = jnp.where(lane_lo, 0, 1)

    def bias_rows(r, head):
        start = _na_window_start(r, rows)
        blocks = []
        for p in range(npairs):
            kr = krow0 + 2 * p
            dr = jnp.clip(kr - r + NA_ROWS - 1, 0, 2 * NA_ROWS - 3)
            krow = kr + pair_row
            ok = jnp.logical_and(krow >= start, krow < start + NA_ROWS)
            blocks.append(jnp.where(ok, bias_ref[head, dr], neg))
        return jnp.concatenate(blocks, axis=1)

    s = []
    for hp in range(heads // 2):
        pcols = slice(hp * pw, (hp + 1) * pw)
        qs, bs = [], []
        for i in range(NA_GROUP):
            q = q_ref[i * GRID_W:(i + 1) * GRID_W, pcols] * scale
            for half in range(2):
                qs.append(jnp.where(lane_lo if half == 0 else jnp.logical_not(lane_lo), q,
                                    jnp.zeros_like(q)))
                bs.append(bias_rows(g * NA_GROUP + i, 2 * hp + half))
        k = k_ref[pl.ds(tok0, nkeys), pcols]
        s.append(lax.dot_general(jnp.concatenate(qs, axis=0), k, _NT,
                                 preferred_element_type=F32) + jnp.concatenate(bs, axis=0))
    p = [jnp.exp(x - jnp.max(x, axis=1, keepdims=True)).astype(BF16) for x in s]
    outs = []
    for hp, x in enumerate(p):
        v_aug = jnp.concatenate([v_ref[pl.ds(tok0, nkeys), hp * pw:(hp + 1) * pw], ones], axis=1)
        res = jnp.dot(x, v_aug, preferred_element_type=F32)
        per_row = []
        for i in range(NA_GROUP):
            lo = res[(2 * i) * GRID_W:(2 * i + 1) * GRID_W]
            hi = res[(2 * i + 1) * GRID_W:(2 * i + 2) * GRID_W]
            both = jnp.where(jnp.concatenate([lane_lo, lane_lo], axis=1), lo, hi)
            per_row.append(both[:, :pw] / both[:, pw:])
        outs.append(jnp.concatenate(per_row, axis=0))
    o_ref[...] = jnp.concatenate(outs, axis=1).astype(o_ref.dtype)


def _na(nq, nk, nv, l, bias_tab, batch, heads):
    m, width = nq.shape
    t = m // batch
    rows = t // GRID_W
    dh = width // heads
    steps = rows // NA_GROUP
    qblk = pl.BlockSpec((NA_GROUP * GRID_W, width), lambda b, r: (b * steps + r, 0))
    return pl.pallas_call(
        functools.partial(_na_kernel, rows=rows, heads=heads, dh=dh),
        grid=(batch, steps),
        in_specs=[qblk,
                  pl.BlockSpec((t, width), lambda b, r: (b, 0)),
                  pl.BlockSpec((t, width), lambda b, r: (b, 0)),
                  pl.BlockSpec((None, heads, 2 * NA_ROWS - 2, GRID_W, 2 * GRID_W),
                               lambda b, r: (l, 0, 0, 0, 0))],
        out_specs=qblk,
        out_shape=jax.ShapeDtypeStruct((m, width), BF16),
        compiler_params=_params("parallel", "arbitrary"),
        name="neighborhood_attn",
    )(nq, nk, nv, bias_tab)


def _kvproj_kernel(mem_ref, w_ref, o_ref):
    o_ref[0] = jnp.dot(mem_ref[...].astype(BF16), w_ref[0],
                       preferred_element_type=F32).astype(o_ref.dtype)


def _kvproj(mem, w_kv):
    mm, d = mem.shape
    depth, _, n = w_kv.shape
    return pl.pallas_call(
        _kvproj_kernel,
        grid=(depth,),
        in_specs=[pl.BlockSpec((mm, d), lambda l: (0, 0)),
                  pl.BlockSpec((1, d, n), lambda l: (l, 0, 0))],
        out_specs=pl.BlockSpec((1, mm, n), lambda l: (l, 0, 0)),
        out_shape=jax.ShapeDtypeStruct((depth, mm, n), BF16),
        compiler_params=_params("parallel"),
        name="memory_kv_proj",
    )(mem, w_kv)


XA_ROW_SPLIT = 2


def _mix_xattn_kernel(h_ref, hf_ref, hb_ref, mo_ref, na_ref, nw_ref, w_ref, b_ref, l1w_ref, l1b_ref,
                      kv_ref, wq_ref, wo_ref, bo_ref, l2w_ref, l2b_ref, o_ref,
                      *, ml_heads, xa_heads, alpha):
    d = h_ref.shape[1]
    width = hf_ref.shape[1]
    mdh = width // ml_heads
    dh = d // xa_heads
    scale = dh ** -0.5
    rg = h_ref.shape[0] // XA_ROW_SPLIT
    groups = [slice(r * rg, (r + 1) * rg) for r in range(XA_ROW_SPLIT)]

    def mixer_out(rows):
        hm = hf_ref[rows, :].astype(F32) + hb_ref[rows, :].astype(F32)
        parts = []
        for hd in range(ml_heads):
            x = hm[:, hd * mdh:(hd + 1) * mdh]
            mu = jnp.mean(x, axis=-1, keepdims=True)
            xc = x - mu
            var = jnp.mean(xc * xc, axis=-1, keepdims=True)
            parts.append(xc * lax.rsqrt(var + LN_EPS))
        hn = jnp.concatenate(parts, axis=1)
        ml = jax.nn.sigmoid(mo_ref[rows, :].astype(F32)) * (hn * nw_ref[...])
        y = (jnp.dot(ml.astype(BF16), w_ref[0:width, :], preferred_element_type=F32)
             + jnp.dot(na_ref[rows, :], w_ref[width:, :], preferred_element_type=F32) + b_ref[...])
        return _layer_norm(alpha * h_ref[rows, :] + y, l1w_ref[...], l1b_ref[...])

    h1 = [mixer_out(rows) for rows in groups]
    q = [jnp.dot(x.astype(BF16), wq_ref[...], preferred_element_type=F32).astype(BF16) for x in h1]
    s = [[lax.dot_general(qr[:, hd * dh:(hd + 1) * dh], kv_ref[:, hd * dh:(hd + 1) * dh], _NT,
                          preferred_element_type=F32) * scale for hd in range(xa_heads)] for qr in q]
    p = [[jnp.exp(x - jnp.max(x, axis=1, keepdims=True)) for x in sr] for sr in s]
    o = [jnp.concatenate(
        [(jnp.dot(x.astype(BF16), kv_ref[:, d + hd * dh:d + (hd + 1) * dh],
                  preferred_element_type=F32) / jnp.sum(x, axis=1, keepdims=True)).astype(BF16)
         for hd, x in enumerate(pr)], axis=1) for pr in p]
    y = [jnp.dot(orow, wo_ref[...], preferred_element_type=F32) + bo_ref[...] for orow in o]
    for rows, x, yr in zip(groups, h1, y):
        o_ref[rows, :] = _layer_norm(alpha * x + yr, l2w_ref[...], l2b_ref[...])


def _mix_xattn(h, hf, hb, mo, na, l, nw, w, b, l1w, l1b, kv, wq, wo, bo, l2w, l2b,
               batch, tm, ml_heads, xa_heads, alpha):
    m, d = h.shape
    width = hf.shape[1]
    nt = m // batch // tm
    mem_len = kv.shape[1] // batch
    row = lambda bi, i: (bi * nt + i, 0)
    layer = lambda bi, i: (l, 0, 0)
    vec = pl.BlockSpec((None, 1, d), layer)
    mat = pl.BlockSpec((None, d, d), layer)
    return pl.pallas_call(
        functools.partial(_mix_xattn_kernel, ml_heads=ml_heads, xa_heads=xa_heads, alpha=alpha),
        grid=(batch, nt),
        in_specs=[pl.BlockSpec((tm, d), row)] + [pl.BlockSpec((tm, width), row)] * 4
                 + [pl.BlockSpec((None, 1, width), layer), mat, vec, vec, vec,
                    pl.BlockSpec((None, mem_len, 2 * d), lambda bi, i: (l, bi, 0)),
                    mat, mat, vec, vec, vec],
        out_specs=pl.BlockSpec((tm, d), row),
        out_shape=jax.ShapeDtypeStruct((m, d), F32),
        compiler_params=_params("parallel", "parallel"),
        name="mixer_out_xattn_ln",
    )(h, hf, hb, mo, na, nw, w, b, l1w, l1b, kv, wq, wo, bo, l2w, l2b)


FFN_HALO = 16
FFN_SUB = 256


def _ffn_kernel(h_ref, hp_ref, hn_ref, w_ref, b_ref, dw_ref, cb_ref, wd_ref, bd_ref, lw_ref, lb_ref,
                o_ref, hb_ref, x_ref, y_ref, *, alpha):
    i = pl.program_id(1)
    tm = h_ref.shape[0]
    hal = FFN_HALO
    sb = FFN_SUB
    dff = y_ref.shape[1]
    nsub = dff // sb
    first = i == 0
    last = i == pl.num_programs(1) - 1

    hb_ref[0:hal, :] = hp_ref[...].astype(BF16)
    hb_ref[hal:hal + tm, :] = h_ref[...].astype(BF16)
    hb_ref[hal + tm:, :] = hn_ref[...].astype(BF16)

    def sides(c):
        return ((slice(0, sb), slice(c * sb, (c + 1) * sb)),
                (slice(sb, 2 * sb), slice(dff + c * sb, dff + (c + 1) * sb)))

    def project(c):
        for lanes, cols in sides(c):
            x = jnp.dot(hb_ref[...], w_ref[:, cols], preferred_element_type=F32)
            nb = -b_ref[:, cols]
            x_ref[c % 2, 0:hal, lanes] = jnp.where(first, nb, x[0:hal])
            x_ref[c % 2, hal:hal + tm, lanes] = x[hal:hal + tm]
            x_ref[c % 2, hal + tm:, lanes] = jnp.where(last, nb, x[hal + tm:])

    def gate(c):
        xs = x_ref.at[c % 2]
        conv = []
        for lanes, cols in sides(c):
            taps = [dw_ref[k:k + 1, cols] for k in range(3)]
            bias = b_ref[:, cols] * (taps[0] + taps[1] + taps[2]) + cb_ref[:, cols]
            conv.append(taps[0] * xs[hal - 1:hal - 1 + tm, lanes] + taps[1] * xs[hal:hal + tm, lanes]
                        + taps[2] * xs[hal + 1:hal + 1 + tm, lanes] + bias)
        g, u = conv
        y = (0.5 * g * (1.0 + lax.erf(g * (2.0 ** -0.5)))) * u
        y_ref[:, c * sb:(c + 1) * sb] = y.astype(BF16)

    project(0)
    for c in range(nsub):
        if c + 1 < nsub:
            project(c + 1)
        gate(c)
    down = jnp.dot(y_ref[...], wd_ref[...], preferred_element_type=F32)
    o_ref[...] = _layer_norm(alpha * h_ref[...] + down + bd_ref[...], lw_ref[...], lb_ref[...])


def _ffn(h, l, w_up, b_up, w_dw, b_dw, w_down, b_down, lw, lb, batch, tm, alpha):
    m, d = h.shape
    dff = w_down.shape[1]
    t = m // batch
    nt = t // tm
    hal = FFN_HALO
    row = lambda b, i: (b * nt + i, 0)
    prev = lambda b, i: (jnp.maximum((b * t + i * tm) // hal - 1, 0), 0)
    nxt = lambda b, i: (jnp.minimum((b * t + (i + 1) * tm) // hal, m // hal - 1), 0)
    layer = lambda b, i: (l, 0, 0)
    resident = dict(pipeline_mode=pl.Buffered(1))
    return pl.pallas_call(
        functools.partial(_ffn_kernel, alpha=alpha),
        grid=(batch, nt),
        in_specs=[pl.BlockSpec((tm, d), row), pl.BlockSpec((hal, d), prev), pl.BlockSpec((hal, d), nxt),
                  pl.BlockSpec((None, d, 2 * dff), layer, **resident),
                  pl.BlockSpec((None, 1, 2 * dff), layer),
                  pl.BlockSpec((None, 3, 2 * dff), layer),
                  pl.BlockSpec((None, 1, 2 * dff), layer),
                  pl.BlockSpec((None, dff, d), layer, **resident)]
                 + [pl.BlockSpec((None, 1, d), layer)] * 3,
        out_specs=pl.BlockSpec((tm, d), row),
        out_shape=jax.ShapeDtypeStruct((m, d), F32),
        scratch_shapes=[pltpu.VMEM((tm + 2 * hal, d), BF16),
                        pltpu.VMEM((2, tm + 2 * hal, 2 * FFN_SUB), F32),
                        pltpu.VMEM((tm, dff), BF16)],
        compiler_params=_params("parallel", "arbitrary"),
        name="conv_ffn_ln",
    )(h, h, h, w_up, b_up, w_dw, b_dw, w_down, b_down, lw, lb)


def _pick(n, prefs):
    for p in prefs:
        if n % p == 0:
            return p
    return n


def kernel(x, mem, ln_in_w, ln_in_b, w_in, b_in, ml_norm_w, na_rpb, w_mix_out, b_mix_out,
           ln1_w, ln1_b, w_xq, w_xkv, w_xo, b_xo, ln2_w, ln2_b,
           w_up, b_up, w_dw, b_dw, w_down, b_down, ln3_w, ln3_b):
    batch, t, d = x.shape
    depth = w_in.shape[0]
    m = batch * t
    ml_width = d // 2
    na_width = d - ml_width
    dff = w_down.shape[1]
    alpha = (2.0 * depth) ** 0.25
    assert ml_width == na_width and t % (GRID_W * NA_ROWS) == 0 and t % ML_CHUNK == 0

    tm = _pick(t, (512, 256, 128))
    tm_ffn = _pick(t, (1024, 512, 256, 128))

    row2 = lambda a: a.reshape(1, -1)
    row3 = lambda a: a[:, None, :]
    col3 = lambda a: a[:, :, None]
    bf = lambda a: a.astype(BF16)
    k0, k1 = ml_width, 2 * ml_width
    g0 = 4 * ml_width
    g1 = g0 + 4 * ML_HEADS
    w_main = bf(jnp.concatenate([w_in[:, :, :k0], w_in[:, :, k1:g0], w_in[:, :, g1:]], axis=2))
    b_main = row3(jnp.concatenate([b_in[:, :k0], b_in[:, k1:g0], b_in[:, g1:]], axis=1))
    wk_t = jnp.swapaxes(bf(w_in[:, :, k0:k1]), 1, 2)
    bk = col3(b_in[:, k0:k1])
    wg_t = jnp.swapaxes(bf(w_in[:, :, g0:g1]), 1, 2)
    bg = col3(b_in[:, g0:g1])
    bias_tab = jax.vmap(_na_bias_table)(na_rpb)
    w_out, w_q, w_o = bf(w_mix_out), bf(w_xq), bf(w_xo)
    w_upb, w_dn = bf(w_up), bf(w_down)

    h = x.reshape(m, d)
    kv_all = _kvproj(mem.reshape(-1, d), bf(w_xkv))

    for l in range(depth):
        first = (row2(ln_in_w), row2(ln_in_b)) if l == 0 else None
        mq, mv, mo, nq, nk, nv, mk_t, gates, *hn = _inproj(h, l, w_main, b_main, wk_t, bk, wg_t, bg,
                                                          tm, ml_width, input_ln=first)
        h = hn[0] if hn else h
        hf, hb = _mlstm(mq, mk_t, mv, gates, batch, ML_HEADS)
        na = _na(nq, nk, nv, l, bias_tab, batch, NA_HEADS)
        h = _mix_xattn(h, hf, hb, mo, na, l, row3(ml_norm_w), w_out, row3(b_mix_out),
                       row3(ln1_w), row3(ln1_b), kv_all, w_q, w_o, row3(b_xo),
                       row3(ln2_w), row3(ln2_b), batch, tm, ML_HEADS, XA_HEADS, alpha)
        h = _ffn(h, l, w_upb, row3(b_up), w_dw, row3(b_dw), w_dn, row3(b_down), row3(ln3_w), row3(ln3_b),
                 batch, tm_ffn, alpha)
    return h.reshape(batch, t, d)
```

```python
import functools
import math

import jax
import jax.numpy as jnp
from jax import lax
from jax.experimental import pallas as pl
from jax.experimental.pallas import tpu as pltpu

F32 = jnp.float32
BF16 = jnp.bfloat16

LN_EPS = 1e-5
GRID_W = 64
ML_HEADS = 4
ML_CHUNK = 128
NA_HEADS = 8
NA_ROWS = 8
NA_COLS = 16
XA_HEADS = 4

VMEM_LIMIT_BYTES = 56 * 1024 * 1024

_NT = (((1,), (1,)), ((), ()))
_TT = (((0,), (1,)), ((), ()))


def _params(*sem):
    return pltpu.CompilerParams(dimension_semantics=sem, vmem_limit_bytes=VMEM_LIMIT_BYTES)


def _layer_norm(x, w, b):
    mu = jnp.mean(x, axis=-1, keepdims=True)
    xc = x - mu
    var = jnp.mean(xc * xc, axis=-1, keepdims=True)
    return xc * lax.rsqrt(var + LN_EPS) * w + b


def _scan_chunks(x, reverse, op, fill):
    n = x.shape[-1]
    pos = lax.broadcasted_iota(jnp.int32, x.shape, 1) % ML_CHUNK
    k = 1
    while k < ML_CHUNK:
        if reverse:
            x = op(x, jnp.where(pos < ML_CHUNK - k, pltpu.roll(x, n - k, axis=1), fill))
        else:
            x = op(x, jnp.where(pos >= k, pltpu.roll(x, k, axis=1), fill))
        k *= 2
    return x


def _inproj_kernel(*refs, width, input_norm):
    if input_norm:
        h_ref, lnw_ref, lnb_ref, *refs = refs
        *refs, hn_ref = refs
    else:
        h_ref, *refs = refs
    (w_ref, b_ref, wk_ref, bk_ref, wg_ref, bg_ref,
     mq_ref, mv_ref, mo_ref, nq_ref, nk_ref, nv_ref, kt_ref, g_ref) = refs
    h = h_ref[...]
    if input_norm:
        h = _layer_norm(h, lnw_ref[...], lnb_ref[...])
        hn_ref[...] = h
    hb = h.astype(BF16)
    g = lax.dot_general(wg_ref[...], hb, _TT, preferred_element_type=F32) + bg_ref[...]
    heads = g.shape[0] // 4
    rows = []
    for d in range(2):
        ig = g[2 * d * heads:(2 * d + 1) * heads, :]
        lf = jax.nn.log_sigmoid(g[(2 * d + 1) * heads:(2 * d + 2) * heads, :])
        a = _scan_chunks(lf, d == 1, jnp.add, 0.0)
        b = ig - a
        rows += [a, b, _scan_chunks(b, d == 1, jnp.maximum, -jnp.inf)]
    g_ref[...] = jnp.concatenate(rows, axis=0)
    outs = (mq_ref, mv_ref, mo_ref, nq_ref, nk_ref, nv_ref)
    for j, o_ref in enumerate(outs):
        cols = slice(j * width, (j + 1) * width)
        acc = jnp.dot(hb, w_ref[:, cols], preferred_element_type=F32) + b_ref[:, cols]
        o_ref[...] = acc.astype(o_ref.dtype)
    kt = lax.dot_general(wk_ref[...], hb, _TT, preferred_element_type=F32) + bk_ref[...]
    kt_ref[...] = kt.astype(kt_ref.dtype)


def _inproj(h, l, w_main, b_main, w_key, bk, w_gate, bg, tm, width, input_ln=None):
    m, d = h.shape
    ncol = w_main.shape[2]
    ng = w_gate.shape[2]
    ns = 6 * (ng // 4)
    row = lambda i: (i, 0)
    col = lambda i: (0, i)
    layer = lambda i: (l, 0, 0)
    seg = lambda dt: jax.ShapeDtypeStruct((m, width), dt)
    seg_spec = pl.BlockSpec((tm, width), row)
    fused = input_ln is not None
    ln_specs = [pl.BlockSpec((1, d), lambda i: (0, 0))] * 2 if fused else []
    ln_args = list(input_ln) if fused else []
    return pl.pallas_call(
        functools.partial(_inproj_kernel, width=width, input_norm=fused),
        grid=(m // tm,),
        in_specs=[pl.BlockSpec((tm, d), row)] + ln_specs + [
                  pl.BlockSpec((None, d, ncol), layer),
                  pl.BlockSpec((None, 1, ncol), layer),
                  pl.BlockSpec((None, d, width), layer),
                  pl.BlockSpec((None, width, 1), layer),
                  pl.BlockSpec((None, d, ng), layer),
                  pl.BlockSpec((None, ng, 1), layer)],
        out_specs=([seg_spec] * 6 + [pl.BlockSpec((width, tm), col), pl.BlockSpec((ns, tm), col)]
                   + ([pl.BlockSpec((tm, d), row)] if fused else [])),
        out_shape=([seg(BF16)] * 6 + [
                   jax.ShapeDtypeStruct((width, m), BF16), jax.ShapeDtypeStruct((ns, m), F32)]
                   + ([jax.ShapeDtypeStruct((m, d), F32)] if fused else [])),
        compiler_params=_params("parallel"),
        name="mixer_inproj",
    )(h, *ln_args, w_main, b_main, w_key, bk, w_gate, bg)


ML_BATCH_PER_STEP = 2


def _mlstm_kernel(*refs, heads, dh):
    L = ML_CHUNK
    scale = dh ** -0.5
    nb = ML_BATCH_PER_STEP
    per_dir = 2 + 2 * nb
    ins, outs, (c_ref, m_ref) = refs[:2 * per_dir], refs[2 * per_dir:2 * per_dir + 2], refs[-2:]

    @pl.when(pl.program_id(1) == 0)
    def _():
        c_ref[...] = jnp.zeros_like(c_ref)
        m_ref[...] = jnp.zeros_like(m_ref)

    t_idx = lax.broadcasted_iota(jnp.int32, (L, L), 0)
    s_idx = lax.broadcasted_iota(jnp.int32, (L, L), 1)

    def col_replicated(x_row):
        return jnp.broadcast_to(x_row, (L, L)).T

    ones = jnp.ones((L, dh), BF16)
    probs = []
    for jd in range(2 * nb):
        d, j = jd // nb, jd % nb
        q_ref, v_ref = ins[d * per_dir].at[j], ins[d * per_dir + 1].at[j]
        kt_ref, g_ref = ins[d * per_dir + 2 + j], ins[d * per_dir + 2 + nb + j]
        o_ref = outs[d].at[j]
        reverse = d == 1
        r0 = 3 * d * heads
        a = g_ref[r0:r0 + heads, :]
        b = g_ref[r0 + heads:r0 + 2 * heads, :]
        b_run = g_ref[r0 + 2 * heads:r0 + 3 * heads, :]
        m_prev = m_ref[jd]
        g_run = jnp.maximum(m_prev, b_run)
        a_end = a[:, 0:1] if reverse else a[:, L - 1:L]
        up_log = a_end + b
        m_new = jnp.maximum(a_end + m_prev, jnp.max(up_log, axis=1, keepdims=True))
        wk = jnp.exp(up_log - m_new) * scale
        decay = jnp.exp(a_end + m_prev - m_new)
        m_ref[jd] = m_new
        causal = (s_idx >= t_idx) if reverse else (s_idx <= t_idx)
        for hd in range(heads):
            row = slice(hd, hd + 1)
            probs.append(dict(
                st=jd * heads + hd, cols=slice(hd * dh, (hd + 1) * dh), causal=causal,
                q_ref=q_ref, kt_ref=kt_ref, v_ref=v_ref, o_ref=o_ref,
                a=a[row], b=b[row], g=g_run[row], m_prev=m_prev[row], wk=wk[row], decay=decay[row]))

    for p in probs:
        p["a_cb"] = col_replicated(p["a"])
        p["g_cb"] = col_replicated(p["g"])
    for p in probs:
        p["q"] = p["q_ref"][:, p["cols"]]
        p["s"] = jnp.dot(p["q"], p["kt_ref"][p["cols"], :], preferred_element_type=F32)
    for p in probs:
        w = jnp.where(p["causal"], jnp.exp(p["b"] - p["g_cb"]), 0.0)
        p["w_qk"] = (w * (p["s"] * scale)).astype(BF16)
        p["inter"] = jnp.exp(p["m_prev"] - p["g_cb"])
        p["floor"] = jnp.exp(-(p["a_cb"] + p["g_cb"]))
    for p in probs:
        p["v_aug"] = jnp.concatenate([p["v_ref"][:, p["cols"]], ones], axis=1)
        p["c_aug"] = c_ref[p["st"]]
        q_c = jnp.dot(p["q"], p["c_aug"].astype(BF16), preferred_element_type=F32)
        w_v = jnp.dot(p["w_qk"], p["v_aug"], preferred_element_type=F32)
        num = w_v[:, :dh] + p["inter"] * q_c[:, :dh]
        den = w_v[:, dh:] + p["inter"] * q_c[:, dh:]
        h_out = num / jnp.maximum(jnp.abs(den), p["floor"])
        p["o_ref"][:, p["cols"]] = h_out.astype(p["o_ref"].dtype)
    for p in probs:
        kw_t = (p["kt_ref"][p["cols"], :].astype(F32) * p["wk"]).astype(BF16)
        decay2 = jnp.concatenate([p["decay"], p["decay"]], axis=1)
        c_ref[p["st"]] = decay2 * p["c_aug"] + jnp.dot(kw_t, p["v_aug"], preferred_element_type=F32)


def _mlstm(mq, mk_t, mv, gates, batch, heads):
    m, width = mq.shape
    t = m // batch
    dh = width // heads
    nc = t // ML_CHUNK
    ng = gates.shape[0]
    nb = ML_BATCH_PER_STEP
    assert batch % nb == 0
    grouped = lambda a: a.reshape(batch // nb, nb, t, width)
    in_specs, operands, out_specs = [], [], []
    for reverse in (False, True):
        chunk = (lambda c: nc - 1 - c) if reverse else (lambda c: c)
        rows = pl.BlockSpec((None, nb, ML_CHUNK, width), lambda bb, c, chunk=chunk: (bb, 0, chunk(c), 0))
        cols = [lambda bb, c, chunk=chunk, j=j: (0, (bb * nb + j) * nc + chunk(c)) for j in range(nb)]
        in_specs += ([rows, rows] + [pl.BlockSpec((width, ML_CHUNK), cm) for cm in cols]
                     + [pl.BlockSpec((ng, ML_CHUNK), cm) for cm in cols])
        operands += [grouped(mq), grouped(mv)] + [mk_t] * nb + [gates] * nb
        out_specs.append(rows)
    hf, hb = pl.pallas_call(
        functools.partial(_mlstm_kernel, heads=heads, dh=dh),
        grid=(batch // nb, nc),
        in_specs=in_specs,
        out_specs=out_specs,
        out_shape=[jax.ShapeDtypeStruct((batch // nb, nb, t, width), BF16)] * 2,
        scratch_shapes=[pltpu.VMEM((2 * nb * heads, dh, 2 * dh), F32),
                        pltpu.VMEM((2 * nb, heads, ML_CHUNK), F32)],
        compiler_params=_params("parallel", "arbitrary"),
        name="mlstm_scan",
    )(*operands)
    return hf.reshape(m, width), hb.reshape(m, width)


def _na_bias_table(rpb):
    q = jnp.arange(GRID_W)[:, None]
    lane = jnp.arange(2 * GRID_W)[None, :]
    k = lane % GRID_W
    dc = jnp.clip(k - q + NA_COLS - 1, 0, 2 * NA_COLS - 2)
    cs = jnp.clip(q - NA_COLS // 2, 0, GRID_W - NA_COLS)
    ok = (k >= cs) & (k < cs + NA_COLS)
    pick = ((lane // GRID_W)[..., None, None] == jnp.arange(2)[:, None]) & (
        dc[..., None, None] == jnp.arange(2 * NA_COLS - 1))
    rows2 = jnp.stack([rpb[:, :-1], rpb[:, 1:]], axis=2).astype(F32)
    tab = jnp.einsum('hdjc,qljc->hdql', rows2, pick.astype(F32), precision=lax.Precision.HIGHEST)
    return jnp.where(ok[None, None], tab, -jnp.inf)


def _na_window_start(r, rows):
    return jnp.clip(r - NA_ROWS // 2, 0, rows - NA_ROWS)


NA_GROUP = 4
NA_KEY_TILES = 3


def _na_kernel(q_ref, k_ref, v_ref, bias_ref, o_ref, *, rows, heads, dh):
    g = pl.program_id(1)
    ktile = NA_GROUP * GRID_W
    nkeys = NA_KEY_TILES * ktile
    scale = dh ** -0.5
    assert math.frexp(scale)[0] == 0.5, "the score scale must be a power of two to fold it into q"
    pw = 2 * dh
    npairs = nkeys // (2 * GRID_W)
    jt = jnp.clip(g - 1, 0, rows // NA_GROUP - NA_KEY_TILES)
    krow0 = jt * NA_GROUP
    tok0 = pl.multiple_of(krow0 * GRID_W, ktile)
    lane_lo = lax.broadcasted_iota(jnp.int32, (GRID_W, pw), 1) < dh
    ones = jnp.ones((nkeys, pw), BF16)
    pair_row = jnp.where(lane_lo[0:1], 0, 1)

    def bias_rows(r, head):
        start = _na_window_start(r, rows)
        blocks = []
        for p in range(npairs):
            kr = krow0 + 2 * p
            dr = jnp.clip(kr - r + NA_ROWS - 1, 0, 2 * NA_ROWS - 3)
            krow = kr + pair_row
            ok = jnp.logical_and(krow >= start, krow < start + NA_ROWS)
            blocks.append(bias_ref[head, dr] + jnp.where(ok, 0.0, -jnp.inf))
        return jnp.concatenate(blocks, axis=1)

    s = []
    for hp in range(heads // 2):
        pcols = slice(hp * pw, (hp + 1) * pw)
        qs, bs = [], []
        for i in range(NA_GROUP):
            q = q_ref[i * GRID_W:(i + 1) * GRID_W, pcols] * scale
            for half in range(2):
                qs.append(jnp.where(lane_lo if half == 0 else jnp.logical_not(lane_lo), q,
                                    jnp.zeros_like(q)))
                bs.append(bias_rows(g * NA_GROUP + i, 2 * hp + half))
        k = k_ref[pl.ds(tok0, nkeys), pcols]
        s.append(lax.dot_general(jnp.concatenate(qs, axis=0), k, _NT,
                                 preferred_element_type=F32) + jnp.concatenate(bs, axis=0))
    p = [jnp.exp(x - jnp.max(x, axis=1, keepdims=True)).astype(BF16) for x in s]
    outs = []
    for hp, x in enumerate(p):
        v_aug = jnp.concatenate([v_ref[pl.ds(tok0, nkeys), hp * pw:(hp + 1) * pw], ones], axis=1)
        res = jnp.dot(x, v_aug, preferred_element_type=F32)
        per_row = []
        for i in range(NA_GROUP):
            lo = res[(2 * i) * GRID_W:(2 * i + 1) * GRID_W]
            hi = res[(2 * i + 1) * GRID_W:(2 * i + 2) * GRID_W]
            both = jnp.where(jnp.concatenate([lane_lo, lane_lo], axis=1), lo, hi)
            per_row.append(both[:, :pw] / both[:, pw:])
        outs.append(jnp.concatenate(per_row, axis=0))
    o_ref[...] = jnp.concatenate(outs, axis=1).astype(o_ref.dtype)


def _na(nq, nk, nv, l, bias_tab, batch, heads):
    m, width = nq.shape
    t = m // batch
    rows = t // GRID_W
    dh = width // heads
    steps = rows // NA_GROUP
    qblk = pl.BlockSpec((NA_GROUP * GRID_W, width), lambda b, r: (b * steps + r, 0))
    return pl.pallas_call(
        functools.partial(_na_kernel, rows=rows, heads=heads, dh=dh),
        grid=(batch, steps),
        in_specs=[qblk,
                  pl.BlockSpec((t, width), lambda b, r: (b, 0)),
                  pl.BlockSpec((t, width), lambda b, r: (b, 0)),
                  pl.BlockSpec((None, heads, 2 * NA_ROWS - 2, GRID_W, 2 * GRID_W),
                               lambda b, r: (l, 0, 0, 0, 0))],
        out_specs=qblk,
        out_shape=jax.ShapeDtypeStruct((m, width), BF16),
        compiler_params=_params("parallel", "arbitrary"),
        name="neighborhood_attn",
    )(nq, nk, nv, bias_tab)


def _kvproj_kernel(mem_ref, w_ref, o_ref):
    o_ref[0] = jnp.dot(mem_ref[...].astype(BF16), w_ref[0],
                       preferred_element_type=F32).astype(o_ref.dtype)


def _kvproj(mem, w_kv):
    mm, d = mem.shape
    depth, _, n = w_kv.shape
    return pl.pallas_call(
        _kvproj_kernel,
        grid=(depth,),
        in_specs=[pl.BlockSpec((mm, d), lambda l: (0, 0)),
                  pl.BlockSpec((1, d, n), lambda l: (l, 0, 0))],
        out_specs=pl.BlockSpec((1, mm, n), lambda l: (l, 0, 0)),
        out_shape=jax.ShapeDtypeStruct((depth, mm, n), BF16),
        compiler_params=_params("parallel"),
        name="memory_kv_proj",
    )(mem, w_kv)


XA_ROW_SPLIT = 2


def _mix_xattn_kernel(h_ref, hf_ref, hb_ref, mo_ref, na_ref, nw_ref, w_ref, b_ref, l1w_ref, l1b_ref,
                      kv_ref, wq_ref, wo_ref, bo_ref, l2w_ref, l2b_ref, o_ref,
                      *, ml_heads, xa_heads, alpha):
    d = h_ref.shape[1]
    width = hf_ref.shape[1]
    mdh = width // ml_heads
    dh = d // xa_heads
    scale = dh ** -0.5
    rg = h_ref.shape[0] // XA_ROW_SPLIT
    groups = [slice(r * rg, (r + 1) * rg) for r in range(XA_ROW_SPLIT)]

    def mixer_out(rows):
        hm = hf_ref[rows, :].astype(F32) + hb_ref[rows, :].astype(F32)
        parts = []
        for hd in range(ml_heads):
            x = hm[:, hd * mdh:(hd + 1) * mdh]
            mu = jnp.mean(x, axis=-1, keepdims=True)
            xc = x - mu
            var = jnp.mean(xc * xc, axis=-1, keepdims=True)
            parts.append(xc * lax.rsqrt(var + LN_EPS))
        hn = jnp.concatenate(parts, axis=1)
        ml = jax.nn.sigmoid(mo_ref[rows, :].astype(F32)) * (hn * nw_ref[...])
        y = (jnp.dot(ml.astype(BF16), w_ref[0:width, :], preferred_element_type=F32)
             + jnp.dot(na_ref[rows, :], w_ref[width:, :], preferred_element_type=F32) + b_ref[...])
        return _layer_norm(alpha * h_ref[rows, :] + y, l1w_ref[...], l1b_ref[...])

    h1 = [mixer_out(rows) for rows in groups]
    q = [jnp.dot(x.astype(BF16), wq_ref[...], preferred_element_type=F32).astype(BF16) for x in h1]
    s = [[lax.dot_general(qr[:, hd * dh:(hd + 1) * dh], kv_ref[:, hd * dh:(hd + 1) * dh], _NT,
                          preferred_element_type=F32) * scale for hd in range(xa_heads)] for qr in q]
    p = [[jnp.exp(x - jnp.max(x, axis=1, keepdims=True)) for x in sr] for sr in s]
    o = [jnp.concatenate(
        [(jnp.dot(x.astype(BF16), kv_ref[:, d + hd * dh:d + (hd + 1) * dh],
                  preferred_element_type=F32) / jnp.sum(x, axis=1, keepdims=True)).astype(BF16)
         for hd, x in enumerate(pr)], axis=1) for pr in p]
    y = [jnp.dot(orow, wo_ref[...], preferred_element_type=F32) + bo_ref[...] for orow in o]
    for rows, x, yr in zip(groups, h1, y):
        o_ref[rows, :] = _layer_norm(alpha * x + yr, l2w_ref[...], l2b_ref[...])


def _mix_xattn(h, hf, hb, mo, na, l, nw, w, b, l1w, l1b, kv, wq, wo, bo, l2w, l2b,
               batch, tm, ml_heads, xa_heads, alpha):
    m, d = h.shape
    width = hf.shape[1]
    nt = m // batch // tm
    mem_len = kv.shape[1] // batch
    row = lambda bi, i: (bi * nt + i, 0)
    layer = lambda bi, i: (l, 0, 0)
    vec = pl.BlockSpec((None, 1, d), layer)
    mat = pl.BlockSpec((None, d, d), layer)
    return pl.pallas_call(
        functools.partial(_mix_xattn_kernel, ml_heads=ml_heads, xa_heads=xa_heads, alpha=alpha),
        grid=(batch, nt),
        in_specs=[pl.BlockSpec((tm, d), row)] + [pl.BlockSpec((tm, width), row)] * 4
                 + [pl.BlockSpec((None, 1, width), layer), mat, vec, vec, vec,
                    pl.BlockSpec((None, mem_len, 2 * d), lambda bi, i: (l, bi, 0)),
                    mat, mat, vec, vec, vec],
        out_specs=pl.BlockSpec((tm, d), row),
        out_shape=jax.ShapeDtypeStruct((m, d), F32),
        compiler_params=_params("parallel", "parallel"),
        name="mixer_out_xattn_ln",
    )(h, hf, hb, mo, na, nw, w, b, l1w, l1b, kv, wq, wo, bo, l2w, l2b)


FFN_HALO = 16
FFN_SUB = 256


def _ffn_kernel(h_ref, hp_ref, hn_ref, w_ref, b_ref, dw_ref, cb_ref, wd_ref, bd_ref, lw_ref, lb_ref,
                o_ref, hb_ref, x_ref, y_ref, *, alpha):
    i = pl.program_id(1)
    tm = h_ref.shape[0]
    hal = FFN_HALO
    sb = FFN_SUB
    dff = y_ref.shape[1]
    nsub = dff // sb
    first = i == 0
    last = i == pl.num_programs(1) - 1

    hb_ref[0:hal, :] = hp_ref[...].astype(BF16)
    hb_ref[hal:hal + tm, :] = h_ref[...].astype(BF16)
    hb_ref[hal + tm:, :] = hn_ref[...].astype(BF16)

    def sides(c):
        return ((slice(0, sb), slice(c * sb, (c + 1) * sb)),
                (slice(sb, 2 * sb), slice(dff + c * sb, dff + (c + 1) * sb)))

    def project(c):
        for lanes, cols in sides(c):
            x = jnp.dot(hb_ref[...], w_ref[:, cols], preferred_element_type=F32)
            nb = -b_ref[:, cols]
            x_ref[c % 2, 0:hal, lanes] = jnp.where(first, nb, x[0:hal])
            x_ref[c % 2, hal:hal + tm, lanes] = x[hal:hal + tm]
            x_ref[c % 2, hal + tm:, lanes] = jnp.where(last, nb, x[hal + tm:])

    def gate(c):
        xs = x_ref.at[c % 2]
        conv = []
        for lanes, cols in sides(c):
            taps = [dw_ref[k:k + 1, cols] for k in range(3)]
            bias = b_ref[:, cols] * (taps[0] + taps[1] + taps[2]) + cb_ref[:, cols]
            conv.append(taps[0] * xs[hal - 1:hal - 1 + tm, lanes] + taps[1] * xs[hal:hal + tm, lanes]
                        + taps[2] * xs[hal + 1:hal + 1 + tm, lanes] + bias)
        g, u = conv
        y = (0.5 * g * (1.0 + lax.erf(g * (2.0 ** -0.5)))) * u
        y_ref[:, c * sb:(c + 1) * sb] = y.astype(BF16)

    project(0)
    for c in range(nsub):
        if c + 1 < nsub:
            project(c + 1)
        gate(c)
    down = jnp.dot(y_ref[...], wd_ref[...], preferred_element_type=F32)
    o_ref[...] = _layer_norm(alpha * h_ref[...] + down + bd_ref[...], lw_ref[...], lb_ref[...])


def _ffn(h, l, w_up, b_up, w_dw, b_dw, w_down, b_down, lw, lb, batch, tm, alpha):
    m, d = h.shape
    dff = w_down.shape[1]
    t = m // batch
    nt = t // tm
    hal = FFN_HALO
    row = lambda b, i: (b * nt + i, 0)
    prev = lambda b, i: (jnp.maximum((b * t + i * tm) // hal - 1, 0), 0)
    nxt = lambda b, i: (jnp.minimum((b * t + (i + 1) * tm) // hal, m // hal - 1), 0)
    layer = lambda b, i: (l, 0, 0)
    resident = dict(pipeline_mode=pl.Buffered(1))
    return pl.pallas_call(
        functools.partial(_ffn_kernel, alpha=alpha),
        grid=(batch, nt),
        in_specs=[pl.BlockSpec((tm, d), row), pl.BlockSpec((hal, d), prev), pl.BlockSpec((hal, d), nxt),
                  pl.BlockSpec((None, d, 2 * dff), layer, **resident),
                  pl.BlockSpec((None, 1, 2 * dff), layer),
                  pl.BlockSpec((None, 3, 2 * dff), layer),
                  pl.BlockSpec((None, 1, 2 * dff), layer),
                  pl.BlockSpec((None, dff, d), layer, **resident)]
                 + [pl.BlockSpec((None, 1, d), layer)] * 3,
        out_specs=pl.BlockSpec((tm, d), row),
        out_shape=jax.ShapeDtypeStruct((m, d), F32),
        scratch_shapes=[pltpu.VMEM((tm + 2 * hal, d), BF16),
                        pltpu.VMEM((2, tm + 2 * hal, 2 * FFN_SUB), F32),
                        pltpu.VMEM((tm, dff), BF16)],
        compiler_params=_params("parallel", "arbitrary"),
        name="conv_ffn_ln",
    )(h, h, h, w_up, b_up, w_dw, b_dw, w_down, b_down, lw, lb)


def _pick(n, prefs):
    for p in prefs:
        if n % p == 0:
            return p
    return n


def kernel(x, mem, ln_in_w, ln_in_b, w_in, b_in, ml_norm_w, na_rpb, w_mix_out, b_mix_out,
           ln1_w, ln1_b, w_xq, w_xkv, w_xo, b_xo, ln2_w, ln2_b,
           w_up, b_up, w_dw, b_dw, w_down, b_down, ln3_w, ln3_b):
    batch, t, d = x.shape
    depth = w_in.shape[0]
    m = batch * t
    ml_width = d // 2
    na_width = d - ml_width
    dff = w_down.shape[1]
    alpha = (2.0 * depth) ** 0.25
    assert ml_width == na_width and t % (GRID_W * NA_ROWS) == 0 and t % ML_CHUNK == 0

    tm = _pick(t, (512, 256, 128))
    tm_ffn = _pick(t, (1024, 512, 256, 128))

    row2 = lambda a: a.reshape(1, -1)
    row3 = lambda a: a[:, None, :]
    col3 = lambda a: a[:, :, None]
    bf = lambda a: a.astype(BF16)
    k0, k1 = ml_width, 2 * ml_width
    g0 = 4 * ml_width
    g1 = g0 + 4 * ML_HEADS
    w_main = bf(jnp.concatenate([w_in[:, :, :k0], w_in[:, :, k1:g0], w_in[:, :, g1:]], axis=2))
    b_main = row3(jnp.concatenate([b_in[:, :k0], b_in[:, k1:g0], b_in[:, g1:]], axis=1))
    w_key = bf(w_in[:, :, k0:k1])
    bk = col3(b_in[:, k0:k1])
    w_gate = bf(w_in[:, :, g0:g1])
    bg = col3(b_in[:, g0:g1])
    bias_tab = jax.vmap(_na_bias_table)(na_rpb)
    w_out, w_q, w_o = bf(w_mix_out), bf(w_xq), bf(w_xo)
    w_upb, w_dn = bf(w_up), bf(w_down)

    h = x.reshape(m, d)
    kv_all = _kvproj(mem.reshape(-1, d), bf(w_xkv))

    for l in range(depth):
        first = (row2(ln_in_w), row2(ln_in_b)) if l == 0 else None
        mq, mv, mo, nq, nk, nv, mk_t, gates, *hn = _inproj(h, l, w_main, b_main, w_key, bk, w_gate, bg,
                                                          tm, ml_width, input_ln=first)
        h = hn[0] if hn else h
        hf, hb = _mlstm(mq, mk_t, mv, gates, batch, ML_HEADS)
        na = _na(nq, nk, nv, l, bias_tab, batch, NA_HEADS)
        h = _mix_xattn(h, hf, hb, mo, na, l, row3(ml_norm_w), w_out, row3(b_mix_out),
                       row3(ln1_w), row3(ln1_b), kv_all, w_q, w_o, row3(b_xo),
                       row3(ln2_w), row3(ln2_b), batch, tm, ML_HEADS, XA_HEADS, alpha)
        h = _ffn(h, l, w_upb, row3(b_up), w_dw, row3(b_dw), w_dn, row3(b_down), row3(ln3_w), row3(ln3_b),
                 batch, tm_ffn, alpha)
    return h.reshape(batch, t, d)
```

```python
import functools
import math

import jax
import jax.numpy as jnp
from jax import lax
from jax.experimental import pallas as pl
from jax.experimental.pallas import tpu as pltpu

F32 = jnp.float32
BF16 = jnp.bfloat16

LN_EPS = 1e-5
GRID_W = 64
ML_HEADS = 4
ML_CHUNK = 128
NA_HEADS = 8
NA_ROWS = 8
NA_COLS = 16
XA_HEADS = 4

VMEM_LIMIT_BYTES = 56 * 1024 * 1024

_NT = (((1,), (1,)), ((), ()))
_TT = (((0,), (1,)), ((), ()))


def _params(*sem):
    return pltpu.CompilerParams(dimension_semantics=sem, vmem_limit_bytes=VMEM_LIMIT_BYTES)


def _layer_norm(x, w, b):
    mu = jnp.mean(x, axis=-1, keepdims=True)
    xc = x - mu
    var = jnp.mean(xc * xc, axis=-1, keepdims=True)
    return xc * lax.rsqrt(var + LN_EPS) * w + b


def _scan_chunks(x, reverse, op, fill):
    n = x.shape[-1]
    pos = lax.broadcasted_iota(jnp.int32, x.shape, 1) % ML_CHUNK
    k = 1
    while k < ML_CHUNK:
        if reverse:
            x = op(x, jnp.where(pos < ML_CHUNK - k, pltpu.roll(x, n - k, axis=1), fill))
        else:
            x = op(x, jnp.where(pos >= k, pltpu.roll(x, k, axis=1), fill))
        k *= 2
    return x


def _inproj_kernel(*refs, width, input_norm):
    if input_norm:
        h_ref, lnw_ref, lnb_ref, *refs = refs
        *refs, hn_ref = refs
    else:
        h_ref, *refs = refs
    (w_ref, b_ref, bk_ref, wg_ref, bg_ref,
     mq_ref, mv_ref, mo_ref, nq_ref, nk_ref, nv_ref, kt_ref, g_ref) = refs
    h = h_ref[...]
    if input_norm:
        h = _layer_norm(h, lnw_ref[...], lnb_ref[...])
        hn_ref[...] = h
    hb = h.astype(BF16)
    g = lax.dot_general(wg_ref[...], hb, _TT, preferred_element_type=F32) + bg_ref[...]
    heads = g.shape[0] // 4
    rows = []
    for d in range(2):
        ig = g[2 * d * heads:(2 * d + 1) * heads, :]
        lf = jax.nn.log_sigmoid(g[(2 * d + 1) * heads:(2 * d + 2) * heads, :])
        a = _scan_chunks(lf, d == 1, jnp.add, 0.0)
        b = ig - a
        rows += [a, b, _scan_chunks(b, d == 1, jnp.maximum, -jnp.inf)]
    g_ref[...] = jnp.concatenate(rows, axis=0)
    outs = ((0, mq_ref), (2, mv_ref), (3, mo_ref), (4, nq_ref), (5, nk_ref), (6, nv_ref))
    for j, o_ref in outs:
        cols = slice(j * width, (j + 1) * width)
        acc = jnp.dot(hb, w_ref[:, cols], preferred_element_type=F32) + b_ref[:, cols]
        o_ref[...] = acc.astype(o_ref.dtype)
    kt = lax.dot_general(w_ref[:, width:2 * width], hb, _TT, preferred_element_type=F32) + bk_ref[...]
    kt_ref[...] = kt.astype(kt_ref.dtype)


def _inproj(h, l, w_main, b_main, bk, w_gate, bg, tm, width, input_ln=None):
    m, d = h.shape
    ncol = w_main.shape[2]
    ng = w_gate.shape[2]
    ns = 6 * (ng // 4)
    row = lambda i: (i, 0)
    col = lambda i: (0, i)
    layer = lambda i: (l, 0, 0)
    seg = lambda dt: jax.ShapeDtypeStruct((m, width), dt)
    seg_spec = pl.BlockSpec((tm, width), row)
    fused = input_ln is not None
    ln_specs = [pl.BlockSpec((1, d), lambda i: (0, 0))] * 2 if fused else []
    ln_args = list(input_ln) if fused else []
    return pl.pallas_call(
        functools.partial(_inproj_kernel, width=width, input_norm=fused),
        grid=(m // tm,),
        in_specs=[pl.BlockSpec((tm, d), row)] + ln_specs + [
                  pl.BlockSpec((None, d, ncol), layer),
                  pl.BlockSpec((None, 1, ncol), layer),
                  pl.BlockSpec((None, width, 1), layer),
                  pl.BlockSpec((None, d, ng), layer),
                  pl.BlockSpec((None, ng, 1), layer)],
        out_specs=([seg_spec] * 6 + [pl.BlockSpec((width, tm), col), pl.BlockSpec((ns, tm), col)]
                   + ([pl.BlockSpec((tm, d), row)] if fused else [])),
        out_shape=([seg(BF16)] * 6 + [
                   jax.ShapeDtypeStruct((width, m), BF16), jax.ShapeDtypeStruct((ns, m), F32)]
                   + ([jax.ShapeDtypeStruct((m, d), F32)] if fused else [])),
        compiler_params=_params("parallel"),
        name="mixer_inproj",
    )(h, *ln_args, w_main, b_main, bk, w_gate, bg)


ML_BATCH_PER_STEP = 4


def _mlstm_kernel(*refs, heads, dh):
    L = ML_CHUNK
    scale = dh ** -0.5
    nb = ML_BATCH_PER_STEP
    per_dir = 2 + 2 * nb
    ins, outs, (c_ref, m_ref) = refs[:2 * per_dir], refs[2 * per_dir:2 * per_dir + 2], refs[-2:]

    @pl.when(pl.program_id(1) == 0)
    def _():
        c_ref[...] = jnp.zeros_like(c_ref)
        m_ref[...] = jnp.zeros_like(m_ref)

    t_idx = lax.broadcasted_iota(jnp.int32, (L, L), 0)
    s_idx = lax.broadcasted_iota(jnp.int32, (L, L), 1)

    def col_replicated(x_row):
        return jnp.broadcast_to(x_row, (L, L)).T

    ones = jnp.ones((L, dh), BF16)
    probs = []
    for jd in range(2 * nb):
        d, j = jd // nb, jd % nb
        q_ref, v_ref = ins[d * per_dir].at[j], ins[d * per_dir + 1].at[j]
        kt_ref, g_ref = ins[d * per_dir + 2 + j], ins[d * per_dir + 2 + nb + j]
        o_ref = outs[d].at[j]
        reverse = d == 1
        r0 = 3 * d * heads
        a = g_ref[r0:r0 + heads, :]
        b = g_ref[r0 + heads:r0 + 2 * heads, :]
        b_run = g_ref[r0 + 2 * heads:r0 + 3 * heads, :]
        m_prev = m_ref[jd]
        g_run = jnp.maximum(m_prev, b_run)
        a_end = a[:, 0:1] if reverse else a[:, L - 1:L]
        up_log = a_end + b
        m_new = jnp.maximum(a_end + m_prev, jnp.max(up_log, axis=1, keepdims=True))
        wk = jnp.exp(up_log - m_new) * scale
        decay = jnp.exp(a_end + m_prev - m_new)
        m_ref[jd] = m_new
        causal = (s_idx >= t_idx) if reverse else (s_idx <= t_idx)
        for hd in range(heads):
            row = slice(hd, hd + 1)
            probs.append(dict(
                st=jd * heads + hd, cols=slice(hd * dh, (hd + 1) * dh), causal=causal,
                q_ref=q_ref, kt_ref=kt_ref, v_ref=v_ref, o_ref=o_ref,
                a=a[row], b=b[row], g=g_run[row], m_prev=m_prev[row], wk=wk[row], decay=decay[row]))

    for p in probs:
        p["a_cb"] = col_replicated(p["a"])
        p["g_cb"] = col_replicated(p["g"])
    for p in probs:
        p["q"] = p["q_ref"][:, p["cols"]]
        p["s"] = jnp.dot(p["q"], p["kt_ref"][p["cols"], :], preferred_element_type=F32)
    for p in probs:
        w = jnp.where(p["causal"], jnp.exp(p["b"] - p["g_cb"]), 0.0)
        p["w_qk"] = (w * (p["s"] * scale)).astype(BF16)
        p["inter"] = jnp.exp(p["m_prev"] - p["g_cb"])
        p["floor"] = jnp.exp(-(p["a_cb"] + p["g_cb"]))
    for p in probs:
        p["v_aug"] = jnp.concatenate([p["v_ref"][:, p["cols"]], ones], axis=1)
        p["c_aug"] = c_ref[p["st"]]
        q_c = jnp.dot(p["q"], p["c_aug"].astype(BF16), preferred_element_type=F32)
        w_v = jnp.dot(p["w_qk"], p["v_aug"], preferred_element_type=F32)
        num = w_v[:, :dh] + p["inter"] * q_c[:, :dh]
        den = w_v[:, dh:] + p["inter"] * q_c[:, dh:]
        h_out = num / jnp.maximum(jnp.abs(den), p["floor"])
        p["o_ref"][:, p["cols"]] = h_out.astype(p["o_ref"].dtype)
    for p in probs:
        kw_t = (p["kt_ref"][p["cols"], :].astype(F32) * p["wk"]).astype(BF16)
        decay2 = jnp.concatenate([p["decay"], p["decay"]], axis=1)
        c_ref[p["st"]] = decay2 * p["c_aug"] + jnp.dot(kw_t, p["v_aug"], preferred_element_type=F32)


def _mlstm(mq, mk_t, mv, gates, batch, heads):
    m, width = mq.shape
    t = m // batch
    dh = width // heads
    nc = t // ML_CHUNK
    ng = gates.shape[0]
    nb = ML_BATCH_PER_STEP
    assert batch % nb == 0
    grouped = lambda a: a.reshape(batch // nb, nb, t, width)
    in_specs, operands, out_specs = [], [], []
    for reverse in (False, True):
        chunk = (lambda c: nc - 1 - c) if reverse else (lambda c: c)
        rows = pl.BlockSpec((None, nb, ML_CHUNK, width), lambda bb, c, chunk=chunk: (bb, 0, chunk(c), 0))
        cols = [lambda bb, c, chunk=chunk, j=j: (0, (bb * nb + j) * nc + chunk(c)) for j in range(nb)]
        in_specs += ([rows, rows] + [pl.BlockSpec((width, ML_CHUNK), cm) for cm in cols]
                     + [pl.BlockSpec((ng, ML_CHUNK), cm) for cm in cols])
        operands += [grouped(mq), grouped(mv)] + [mk_t] * nb + [gates] * nb
        out_specs.append(rows)
    hf, hb = pl.pallas_call(
        functools.partial(_mlstm_kernel, heads=heads, dh=dh),
        grid=(batch // nb, nc),
        in_specs=in_specs,
        out_specs=out_specs,
        out_shape=[jax.ShapeDtypeStruct((batch // nb, nb, t, width), BF16)] * 2,
        scratch_shapes=[pltpu.VMEM((2 * nb * heads, dh, 2 * dh), F32),
                        pltpu.VMEM((2 * nb, heads, ML_CHUNK), F32)],
        compiler_params=_params("parallel", "arbitrary"),
        name="mlstm_scan",
    )(*operands)
    return hf.reshape(m, width), hb.reshape(m, width)


def _na_bias_table(rpb):
    q = jnp.arange(GRID_W)[:, None]
    lane = jnp.arange(2 * GRID_W)[None, :]
    k = lane % GRID_W
    dc = jnp.clip(k - q + NA_COLS - 1, 0, 2 * NA_COLS - 2)
    cs = jnp.clip(q - NA_COLS // 2, 0, GRID_W - NA_COLS)
    ok = (k >= cs) & (k < cs + NA_COLS)
    pick = ((lane // GRID_W)[..., None, None] == jnp.arange(2)[:, None]) & (
        dc[..., None, None] == jnp.arange(2 * NA_COLS - 1))
    rows2 = jnp.stack([rpb[:, :-1], rpb[:, 1:]], axis=2).astype(F32)
    tab = jnp.einsum('hdjc,qljc->hdql', rows2, pick.astype(F32), precision=lax.Precision.HIGHEST)
    return jnp.where(ok[None, None], tab, -jnp.inf)


def _na_window_start(r, rows):
    return jnp.clip(r - NA_ROWS // 2, 0, rows - NA_ROWS)


NA_GROUP = 4
NA_KEY_TILES = 3


def _na_kernel(q_ref, k_ref, v_ref, bias_ref, o_ref, *, rows, heads, dh):
    g = pl.program_id(1)
    ktile = NA_GROUP * GRID_W
    nkeys = NA_KEY_TILES * ktile
    scale = dh ** -0.5
    assert math.frexp(scale)[0] == 0.5, "the score scale must be a power of two to fold it into q"
    pw = 2 * dh
    npairs = nkeys // (2 * GRID_W)
    jt = jnp.clip(g - 1, 0, rows // NA_GROUP - NA_KEY_TILES)
    krow0 = jt * NA_GROUP
    tok0 = pl.multiple_of(krow0 * GRID_W, ktile)
    lane_lo = lax.broadcasted_iota(jnp.int32, (GRID_W, pw), 1) < dh
    ones = jnp.ones((nkeys, pw), BF16)
    pair_row = jnp.where(lane_lo[0:1], 0, 1)

    def bias_rows(r, head):
        start = _na_window_start(r, rows)
        blocks = []
        for p in range(npairs):
            kr = krow0 + 2 * p
            dr = jnp.clip(kr - r + NA_ROWS - 1, 0, 2 * NA_ROWS - 3)
            krow = kr + pair_row
            ok = jnp.logical_and(krow >= start, krow < start + NA_ROWS)
            blocks.append(bias_ref[head, dr] + jnp.where(ok, 0.0, -jnp.inf))
        return jnp.concatenate(blocks, axis=1)

    s = []
    for hp in range(heads // 2):
        pcols = slice(hp * pw, (hp + 1) * pw)
        qs, bs = [], []
        for i in range(NA_GROUP):
            q = q_ref[i * GRID_W:(i + 1) * GRID_W, pcols] * scale
            for half in range(2):
                qs.append(jnp.where(lane_lo if half == 0 else jnp.logical_not(lane_lo), q,
                                    jnp.zeros_like(q)))
                bs.append(bias_rows(g * NA_GROUP + i, 2 * hp + half))
        k = k_ref[pl.ds(tok0, nkeys), pcols]
        s.append(lax.dot_general(jnp.concatenate(qs, axis=0), k, _NT,
                                 preferred_element_type=F32) + jnp.concatenate(bs, axis=0))
    p = [jnp.exp(x - jnp.max(x, axis=1, keepdims=True)).astype(BF16) for x in s]
    outs = []
    for hp, x in enumerate(p):
        v_aug = jnp.concatenate([v_ref[pl.ds(tok0, nkeys), hp * pw:(hp + 1) * pw], ones], axis=1)
        res = jnp.dot(x, v_aug, preferred_element_type=F32)
        per_row = []
        for i in range(NA_GROUP):
            lo = res[(2 * i) * GRID_W:(2 * i + 1) * GRID_W]
            hi = res[(2 * i + 1) * GRID_W:(2 * i + 2) * GRID_W]
            both = jnp.where(jnp.concatenate([lane_lo, lane_lo], axis=1), lo, hi)
            per_row.append(both[:, :pw] / both[:, pw:])
        outs.append(jnp.concatenate(per_row, axis=0))
    o_ref[...] = jnp.concatenate(outs, axis=1).astype(o_ref.dtype)


def _na(nq, nk, nv, l, bias_tab, batch, heads):
    m, width = nq.shape
    t = m // batch
    rows = t // GRID_W
    dh = width // heads
    steps = rows // NA_GROUP
    qblk = pl.BlockSpec((NA_GROUP * GRID_W, width), lambda b, r: (b * steps + r, 0))
    return pl.pallas_call(
        functools.partial(_na_kernel, rows=rows, heads=heads, dh=dh),
        grid=(batch, steps),
        in_specs=[qblk,
                  pl.BlockSpec((t, width), lambda b, r: (b, 0)),
                  pl.BlockSpec((t, width), lambda b, r: (b, 0)),
                  pl.BlockSpec((None, heads, 2 * NA_ROWS - 2, GRID_W, 2 * GRID_W),
                               lambda b, r: (l, 0, 0, 0, 0))],
        out_specs=qblk,
        out_shape=jax.ShapeDtypeStruct((m, width), BF16),
        compiler_params=_params("parallel", "arbitrary"),
        name="neighborhood_attn",
    )(nq, nk, nv, bias_tab)


def _kvproj_kernel(mem_ref, w_ref, o_ref):
    o_ref[0] = jnp.dot(mem_ref[...].astype(BF16), w_ref[0],
                       preferred_element_type=F32).astype(o_ref.dtype)


def _kvproj(mem, w_kv):
    mm, d = mem.shape
    depth, _, n = w_kv.shape
    return pl.pallas_call(
        _kvproj_kernel,
        grid=(depth,),
        in_specs=[pl.BlockSpec((mm, d), lambda l: (0, 0)),
                  pl.BlockSpec((1, d, n), lambda l: (l, 0, 0))],
        out_specs=pl.BlockSpec((1, mm, n), lambda l: (l, 0, 0)),
        out_shape=jax.ShapeDtypeStruct((depth, mm, n), BF16),
        compiler_params=_params("parallel"),
        name="memory_kv_proj",
    )(mem, w_kv)


XA_ROW_SPLIT = 2


def _mix_xattn_kernel(h_ref, hf_ref, hb_ref, mo_ref, na_ref, nw_ref, w_ref, b_ref, l1w_ref, l1b_ref,
                      kv_ref, wq_ref, wo_ref, bo_ref, l2w_ref, l2b_ref, o_ref,
                      *, ml_heads, xa_heads, alpha):
    d = h_ref.shape[1]
    width = hf_ref.shape[1]
    mdh = width // ml_heads
    dh = d // xa_heads
    scale = dh ** -0.5
    rg = h_ref.shape[0] // XA_ROW_SPLIT
    groups = [slice(r * rg, (r + 1) * rg) for r in range(XA_ROW_SPLIT)]

    def mixer_out(rows):
        hm = hf_ref[rows, :].astype(F32) + hb_ref[rows, :].astype(F32)
        parts = []
        for hd in range(ml_heads):
            x = hm[:, hd * mdh:(hd + 1) * mdh]
            mu = jnp.mean(x, axis=-1, keepdims=True)
            xc = x - mu
            var = jnp.mean(xc * xc, axis=-1, keepdims=True)
            parts.append(xc * lax.rsqrt(var + LN_EPS))
        hn = jnp.concatenate(parts, axis=1)
        ml = jax.nn.sigmoid(mo_ref[rows, :].astype(F32)) * (hn * nw_ref[...])
        y = (jnp.dot(ml.astype(BF16), w_ref[0:width, :], preferred_element_type=F32)
             + jnp.dot(na_ref[rows, :], w_ref[width:, :], preferred_element_type=F32) + b_ref[...])
        return _layer_norm(alpha * h_ref[rows, :] + y, l1w_ref[...], l1b_ref[...])

    h1 = [mixer_out(rows) for rows in groups]
    q = [jnp.dot(x.astype(BF16), wq_ref[...], preferred_element_type=F32).astype(BF16) for x in h1]
    s = [[lax.dot_general(qr[:, hd * dh:(hd + 1) * dh], kv_ref[:, hd * dh:(hd + 1) * dh], _NT,
                          preferred_element_type=F32) * scale for hd in range(xa_heads)] for qr in q]
    p = [[jnp.exp(x - jnp.max(x, axis=1, keepdims=True)) for x in sr] for sr in s]
    o = [jnp.concatenate(
        [(jnp.dot(x.astype(BF16), kv_ref[:, d + hd * dh:d + (hd + 1) * dh],
                  preferred_element_type=F32) / jnp.sum(x, axis=1, keepdims=True)).astype(BF16)
         for hd, x in enumerate(pr)], axis=1) for pr in p]
    y = [jnp.dot(orow, wo_ref[...], preferred_element_type=F32) + bo_ref[...] for orow in o]
    for rows, x, yr in zip(groups, h1, y):
        o_ref[rows, :] = _layer_norm(alpha * x + yr, l2w_ref[...], l2b_ref[...])


def _mix_xattn(h, hf, hb, mo, na, l, nw, w, b, l1w, l1b, kv, wq, wo, bo, l2w, l2b,
               batch, tm, ml_heads, xa_heads, alpha):
    m, d = h.shape
    width = hf.shape[1]
    nt = m // batch // tm
    mem_len = kv.shape[1] // batch
    row = lambda bi, i: (bi * nt + i, 0)
    layer = lambda bi, i: (l, 0, 0)
    vec = pl.BlockSpec((None, 1, d), layer)
    mat = pl.BlockSpec((None, d, d), layer)
    return pl.pallas_call(
        functools.partial(_mix_xattn_kernel, ml_heads=ml_heads, xa_heads=xa_heads, alpha=alpha),
        grid=(batch, nt),
        in_specs=[pl.BlockSpec((tm, d), row)] + [pl.BlockSpec((tm, width), row)] * 4
                 + [pl.BlockSpec((None, 1, width), layer), mat, vec, vec, vec,
                    pl.BlockSpec((None, mem_len, 2 * d), lambda bi, i: (l, bi, 0)),
                    mat, mat, vec, vec, vec],
        out_specs=pl.BlockSpec((tm, d), row),
        out_shape=jax.ShapeDtypeStruct((m, d), F32),
        compiler_params=_params("parallel", "parallel"),
        name="mixer_out_xattn_ln",
    )(h, hf, hb, mo, na, nw, w, b, l1w, l1b, kv, wq, wo, bo, l2w, l2b)


FFN_HALO = 16
FFN_SUB = 256


def _ffn_kernel(h_ref, hp_ref, hn_ref, w_ref, b_ref, dw_ref, cb_ref, wd_ref, bd_ref, lw_ref, lb_ref,
                o_ref, hb_ref, x_ref, y_ref, *, alpha):
    i = pl.program_id(1)
    tm = h_ref.shape[0]
    hal = FFN_HALO
    sb = FFN_SUB
    dff = y_ref.shape[1]
    nsub = dff // sb
    first = i == 0
    last = i == pl.num_programs(1) - 1

    hb_ref[0:hal, :] = hp_ref[...].astype(BF16)
    hb_ref[hal:hal + tm, :] = h_ref[...].astype(BF16)
    hb_ref[hal + tm:, :] = hn_ref[...].astype(BF16)

    def sides(c):
        return ((slice(0, sb), slice(c * sb, (c + 1) * sb)),
                (slice(sb, 2 * sb), slice(dff + c * sb, dff + (c + 1) * sb)))

    def project(c):
        for lanes, cols in sides(c):
            x = jnp.dot(hb_ref[...], w_ref[:, cols], preferred_element_type=F32)
            nb = -b_ref[:, cols]
            x_ref[c % 2, 0:hal, lanes] = jnp.where(first, nb, x[0:hal])
            x_ref[c % 2, hal:hal + tm, lanes] = x[hal:hal + tm]
            x_ref[c % 2, hal + tm:, lanes] = jnp.where(last, nb, x[hal + tm:])

    def gate(c):
        xs = x_ref.at[c % 2]
        conv = []
        for lanes, cols in sides(c):
            taps = [dw_ref[k:k + 1, cols] for k in range(3)]
            bias = b_ref[:, cols] * (taps[0] + taps[1] + taps[2]) + cb_ref[:, cols]
            conv.append(taps[0] * xs[hal - 1:hal - 1 + tm, lanes] + taps[1] * xs[hal:hal + tm, lanes]
                        + taps[2] * xs[hal + 1:hal + 1 + tm, lanes] + bias)
        g, u = conv
        y = (0.5 * g * (1.0 + lax.erf(g * (2.0 ** -0.5)))) * u
        y_ref[:, c * sb:(c + 1) * sb] = y.astype(BF16)

    project(0)
    for c in range(nsub):
        if c + 1 < nsub:
            project(c + 1)
        gate(c)
    down = jnp.dot(y_ref[...], wd_ref[...], preferred_element_type=F32)
    o_ref[...] = _layer_norm(alpha * h_ref[...] + down + bd_ref[...], lw_ref[...], lb_ref[...])


def _ffn(h, l, w_up, b_up, w_dw, b_dw, w_down, b_down, lw, lb, batch, tm, alpha):
    m, d = h.shape
    dff = w_down.shape[1]
    t = m // batch
    nt = t // tm
    hal = FFN_HALO
    row = lambda b, i: (b * nt + i, 0)
    prev = lambda b, i: (jnp.maximum((b * t + i * tm) // hal - 1, 0), 0)
    nxt = lambda b, i: (jnp.minimum((b * t + (i + 1) * tm) // hal, m // hal - 1), 0)
    layer = lambda b, i: (l, 0, 0)
    resident = dict(pipeline_mode=pl.Buffered(1))
    return pl.pallas_call(
        functools.partial(_ffn_kernel, alpha=alpha),
        grid=(batch, nt),
        in_specs=[pl.BlockSpec((tm, d), row), pl.BlockSpec((hal, d), prev), pl.BlockSpec((hal, d), nxt),
                  pl.BlockSpec((None, d, 2 * dff), layer, **resident),
                  pl.BlockSpec((None, 1, 2 * dff), layer),
                  pl.BlockSpec((None, 3, 2 * dff), layer),
                  pl.BlockSpec((None, 1, 2 * dff), layer),
                  pl.BlockSpec((None, dff, d), layer, **resident)]
                 + [pl.BlockSpec((None, 1, d), layer)] * 3,
        out_specs=pl.BlockSpec((tm, d), row),
        out_shape=jax.ShapeDtypeStruct((m, d), F32),
        scratch_shapes=[pltpu.VMEM((tm + 2 * hal, d), BF16),
                        pltpu.VMEM((2, tm + 2 * hal, 2 * FFN_SUB), F32),
                        pltpu.VMEM((tm, dff), BF16)],
        compiler_params=_params("parallel", "arbitrary"),
        name="conv_ffn_ln",
    )(h, h, h, w_up, b_up, w_dw, b_dw, w_down, b_down, lw, lb)


def _pick(n, prefs):
    for p in prefs:
        if n % p == 0:
            return p
    return n


def kernel(x, mem, ln_in_w, ln_in_b, w_in, b_in, ml_norm_w, na_rpb, w_mix_out, b_mix_out,
           ln1_w, ln1_b, w_xq, w_xkv, w_xo, b_xo, ln2_w, ln2_b,
           w_up, b_up, w_dw, b_dw, w_down, b_down, ln3_w, ln3_b):
    batch, t, d = x.shape
    depth = w_in.shape[0]
    m = batch * t
    ml_width = d // 2
    na_width = d - ml_width
    dff = w_down.shape[1]
    alpha = (2.0 * depth) ** 0.25
    assert ml_width == na_width and t % (GRID_W * NA_ROWS) == 0 and t % ML_CHUNK == 0

    tm = _pick(t, (512, 256, 128))
    tm_ffn = _pick(t, (1024, 512, 256, 128))

    row2 = lambda a: a.reshape(1, -1)
    row3 = lambda a: a[:, None, :]
    col3 = lambda a: a[:, :, None]
    bf = lambda a: a.astype(BF16)
    k0, k1 = ml_width, 2 * ml_width
    g0 = 4 * ml_width
    g1 = g0 + 4 * ML_HEADS
    w_main = bf(jnp.concatenate([w_in[:, :, :g0], w_in[:, :, g1:]], axis=2))
    b_main = row3(jnp.concatenate([b_in[:, :g0], b_in[:, g1:]], axis=1))
    bk = col3(b_in[:, k0:k1])
    w_gate = bf(w_in[:, :, g0:g1])
    bg = col3(b_in[:, g0:g1])
    bias_tab = jax.vmap(_na_bias_table)(na_rpb)
    w_out, w_q, w_o = bf(w_mix_out), bf(w_xq), bf(w_xo)
    w_upb, w_dn = bf(w_up), bf(w_down)

    h = x.reshape(m, d)
    kv_all = _kvproj(mem.reshape(-1, d), bf(w_xkv))

    for l in range(depth):
        first = (row2(ln_in_w), row2(ln_in_b)) if l == 0 else None
        mq, mv, mo, nq, nk, nv, mk_t, gates, *hn = _inproj(h, l, w_main, b_main, bk, w_gate, bg,
                                                          tm, ml_width, input_ln=first)
        h = hn[0] if hn else h
        hf, hb = _mlstm(mq, mk_t, mv, gates, batch, ML_HEADS)
        na = _na(nq, nk, nv, l, bias_tab, batch, NA_HEADS)
        h = _mix_xattn(h, hf, hb, mo, na, l, row3(ml_norm_w), w_out, row3(b_mix_out),
                       row3(ln1_w), row3(ln1_b), kv_all, w_q, w_o, row3(b_xo),
                       row3(ln2_w), row3(ln2_b), batch, tm, ML_HEADS, XA_HEADS, alpha)
        h = _ffn(h, l, w_upb, row3(b_up), w_dw, row3(b_dw), w_dn, row3(b_down), row3(ln3_w), row3(ln3_b),
                 batch, tm_ffn, alpha)
    return h.reshape(batch, t, d)
```
